```python
import jax, jax.numpy as jnp
from jax import lax
import numpy as np

D_MODEL = 1024
BATCH = 8
SEQ = 4096
DEPTH = 4
DEC_BATCH = 2
DEC_SEQ = 16384
PAST_LEN = 128

N_MIXERS = 2
D_FF = 2816
CONV_WIDTH = 3
N_HEADS = 8
D_QK = D_MODEL // 2
D_V = D_MODEL
DH_QK = D_QK // N_HEADS
DH_V = D_V // N_HEADS
CHUNK = 64
FORGET_BIAS = 3.0
EPS = 1e-6

kernel_name = "hybrid_shortconv_mlstm_macaron_encoder"


def rmsnorm(x, g):
    xf = x.astype(jnp.float32)
    y = xf * lax.rsqrt(jnp.mean(xf * xf, axis=-1, keepdims=True) + EPS)
    return (y * g.astype(jnp.float32)).astype(x.dtype)


def swiglu(x, w_gate, w_up, w_down):
    return (jax.nn.silu(x @ w_gate) * (x @ w_up)) @ w_down


def short_conv_mixer(x, w_in, w_dw, w_out):
    s = x.shape[1]
    b_gate, c_gate, h = jnp.split(x @ w_in, 3, axis=-1)
    z = c_gate * h
    zp = jnp.pad(z, ((0, 0), (1, 1), (0, 0)))
    y = w_dw[0] * zp[:, :s] + w_dw[1] * zp[:, 1:s + 1] + w_dw[2] * zp[:, 2:]
    return (b_gate * y) @ w_out


def mlstm_chunkwise(q, k, v, i_pre, f_pre):
    bsz, nh, s, dk = q.shape
    dv = v.shape[-1]
    nc = s // CHUNK
    q = q.reshape(bsz, nh, nc, CHUNK, dk)
    k = k.reshape(bsz, nh, nc, CHUNK, dk)
    v = v.reshape(bsz, nh, nc, CHUNK, dv)
    log_f = jax.nn.log_sigmoid(f_pre).reshape(bsz, nh, nc, CHUNK)
    log_i = i_pre.reshape(bsz, nh, nc, CHUNK)
    b = jnp.cumsum(log_f, axis=-1)
    g = b[..., -1]

    a = g[..., None] - b + log_i
    m_loc = jnp.max(a, axis=-1)
    w_loc = jnp.exp(a - m_loc[..., None])
    c_loc = jnp.einsum('bhcl,bhclk,bhclv->bhckv', w_loc, k, v)
    n_loc = jnp.einsum('bhcl,bhclk->bhck', w_loc, k)

    def step(carry, inp):
        c, n, m = carry
        g_c, m_l, c_l, n_l = inp
        m_new = jnp.maximum(g_c + m, m_l)
        s_old = jnp.exp(g_c + m - m_new)
        s_loc = jnp.exp(m_l - m_new)
        c_new = s_old[..., None, None] * c + s_loc[..., None, None] * c_l
        n_new = s_old[..., None] * n + s_loc[..., None] * n_l
        return (c_new, n_new, m_new), (c, n, m)

    init = (jnp.zeros((bsz, nh, dk, dv), jnp.float32),
            jnp.zeros((bsz, nh, dk), jnp.float32),
            jnp.zeros((bsz, nh), jnp.float32))
    xs = (jnp.moveaxis(g, 2, 0), jnp.moveaxis(m_loc, 2, 0),
          jnp.moveaxis(c_loc, 2, 0), jnp.moveaxis(n_loc, 2, 0))
    _, (c_prev, n_prev, m_prev) = lax.scan(step, init, xs)
    c_prev = jnp.moveaxis(c_prev, 0, 2)
    n_prev = jnp.moveaxis(n_prev, 0, 2)
    m_prev = jnp.moveaxis(m_prev, 0, 2)

    d = b[..., :, None] - b[..., None, :] + log_i[..., None, :]
    seen = jnp.tril(jnp.ones((CHUNK, CHUNK), dtype=bool))
    d = jnp.where(seen, d, -jnp.inf)
    m_inter = b + m_prev[..., None]
    m_out = jnp.maximum(m_inter, jnp.max(d, axis=-1))
    sc = jnp.einsum('bhctk,bhcjk->bhctj', q, k) * jnp.exp(d - m_out[..., None])
    w_inter = jnp.exp(m_inter - m_out)
    num = (jnp.einsum('bhctj,bhcjv->bhctv', sc, v)
           + w_inter[..., None] * jnp.einsum('bhctk,bhckv->bhctv', q, c_prev))
    den = jnp.sum(sc, axis=-1) + w_inter * jnp.einsum('bhctk,bhck->bhct', q, n_prev)
    h = num / jnp.maximum(jnp.abs(den), jnp.exp(-m_out))[..., None]
    return h.reshape(bsz, nh, s, dv)


def mlstm_mixer(x, w_in, w_gate, b_gate, g_head, w_out):
    bsz, s, _ = x.shape
    q, k, v, o = jnp.split(x @ w_in, [D_QK, 2 * D_QK, 2 * D_QK + D_V], axis=-1)

    def heads(t, dh):
        return t.reshape(bsz, s, N_HEADS, dh).transpose(0, 2, 1, 3).astype(jnp.float32)

    q = heads(q, DH_QK)
    k = heads(k, DH_QK) * (DH_QK ** -0.5)
    v = heads(v, DH_V)
    gates = x.astype(jnp.float32) @ w_gate.astype(jnp.float32) + b_gate.astype(jnp.float32)
    gates = gates.reshape(bsz, s, 4, N_HEADS).transpose(2, 0, 3, 1)
    i_fw, f_fw, i_bw, f_bw = gates[0], gates[1], gates[2], gates[3]

    h_fw = mlstm_chunkwise(q, k, v, i_fw, f_fw)
    flip = lambda t: jnp.flip(t, axis=2)
    h_bw = flip(mlstm_chunkwise(flip(q), flip(k), flip(v), flip(i_bw), flip(f_bw)))
    h = h_fw + h_bw
    h = h * lax.rsqrt(jnp.mean(h * h, axis=-1, keepdims=True) + EPS)
    h = h.transpose(0, 2, 1, 3).reshape(bsz, s, D_V) * g_head.astype(jnp.float32)
    h = (h * jax.nn.sigmoid(o.astype(jnp.float32))).astype(x.dtype)
    return h @ w_out


def trunk(x, g_ffn1, w_ffn1_gate, w_ffn1_up, w_ffn1_down, g_mix,
          w_conv_in, w_conv_dw, w_conv_out,
          w_mlstm_in, w_mlstm_gate, b_mlstm_gate, g_mlstm_head, w_mlstm_out,
          g_ffn2, w_ffn2_gate, w_ffn2_up, w_ffn2_down, g_final):
    for layer in range(DEPTH):
        j = layer // N_MIXERS
        x = x + 0.5 * swiglu(rmsnorm(x, g_ffn1[layer]), w_ffn1_gate[layer],
                             w_ffn1_up[layer], w_ffn1_down[layer])
        h = rmsnorm(x, g_mix[layer])
        if layer % N_MIXERS == 0:
            x = x + short_conv_mixer(h, w_conv_in[j], w_conv_dw[j], w_conv_out[j])
        else:
            x = x + mlstm_mixer(h, w_mlstm_in[j], w_mlstm_gate[j], b_mlstm_gate[j],
                                g_mlstm_head[j], w_mlstm_out[j])
        x = x + 0.5 * swiglu(rmsnorm(x, g_ffn2[layer]), w_ffn2_gate[layer],
                             w_ffn2_up[layer], w_ffn2_down[layer])
    return rmsnorm(x, g_final)


def setup_inputs(seed: int = 0) -> dict:
    key = jax.random.key(seed)
    ks = jax.random.split(key, 24)
    n_conv = (DEPTH + 1) // 2
    n_ml = DEPTH // 2
    d = D_MODEL
    nrm = lambda k, shape, fan_in: jax.random.normal(k, shape, jnp.float32) * (fan_in ** -0.5)
    gain = lambda k, shape: 1.0 + 0.05 * jax.random.normal(k, shape, jnp.float32)
    gate_base = jnp.tile(jnp.repeat(jnp.array([0.0, FORGET_BIAS], jnp.float32), N_HEADS), 2)
    return {
        "x_prompt": jax.random.normal(ks[0], (BATCH, SEQ, d), jnp.float32),
        "x_sample": jax.random.normal(ks[1], (DEC_BATCH, DEC_SEQ, d), jnp.float32),
        "g_ffn1": gain(ks[2], (DEPTH, d)),
        "w_ffn1_gate": nrm(ks[3], (DEPTH, d, D_FF), d),
        "w_ffn1_up": nrm(ks[4], (DEPTH, d, D_FF), d),
        "w_ffn1_down": nrm(ks[5], (DEPTH, D_FF, d), D_FF),
        "g_mix": gain(ks[6], (DEPTH, d)),
        "w_conv_in": nrm(ks[7], (n_conv, d, 3 * d), d),
        "w_conv_dw": nrm(ks[8], (n_conv, CONV_WIDTH, d), CONV_WIDTH),
        "w_conv_out": nrm(ks[9], (n_conv, d, d), d),
        "w_mlstm_in": nrm(ks[10], (n_ml, d, 2 * D_QK + 2 * D_V), d),
        "w_mlstm_gate": nrm(ks[11], (n_ml, d, 4 * N_HEADS), d),
        "b_mlstm_gate": gate_base + 0.1 * jax.random.normal(ks[12], (n_ml, 4 * N_HEADS), jnp.float32),
        "g_mlstm_head": gain(ks[13], (n_ml, D_V)),
        "w_mlstm_out": nrm(ks[14], (n_ml, D_V, d), D_V),
        "g_ffn2": gain(ks[15], (DEPTH, d)),
        "w_ffn2_gate": nrm(ks[16], (DEPTH, d, D_FF), d),
        "w_ffn2_up": nrm(ks[17], (DEPTH, d, D_FF), d),
        "w_ffn2_down": nrm(ks[18], (DEPTH, D_FF, d), D_FF),
        "g_final": gain(ks[19], (d,)),
    }


def reference(x_prompt, x_sample, g_ffn1, w_ffn1_gate, w_ffn1_up, w_ffn1_down, g_mix,
              w_conv_in, w_conv_dw, w_conv_out,
              w_mlstm_in, w_mlstm_gate, b_mlstm_gate, g_mlstm_head, w_mlstm_out,
              g_ffn2, w_ffn2_gate, w_ffn2_up, w_ffn2_down, g_final):
    y_prompt = trunk(x_prompt, g_ffn1, w_ffn1_gate, w_ffn1_up, w_ffn1_down, g_mix,
                     w_conv_in, w_conv_dw, w_conv_out,
                     w_mlstm_in, w_mlstm_gate, b_mlstm_gate, g_mlstm_head, w_mlstm_out,
                     g_ffn2, w_ffn2_gate, w_ffn2_up, w_ffn2_down, g_final)
    y_sample = trunk(x_sample, g_ffn1, w_ffn1_gate, w_ffn1_up, w_ffn1_down, g_mix,
                     w_conv_in, w_conv_dw, w_conv_out,
                     w_mlstm_in, w_mlstm_gate, b_mlstm_gate, g_mlstm_head, w_mlstm_out,
                     g_ffn2, w_ffn2_gate, w_ffn2_up, w_ffn2_down, g_final)
    return (y_prompt, y_sample)
```

```python
import functools

import jax
import jax.numpy as jnp
from jax import lax
from jax.experimental import pallas as pl
from jax.experimental.pallas import tpu as pltpu

F32 = jnp.float32
BF16 = jnp.bfloat16

N_HEADS = 8
DH_QK = 64
DH_V = 128
EPS = 1e-6

V7X_VMEM_BYTES = 64 * 1024 * 1024
V7X_LANES = 128
BF16_SUBLANES = 16

TOKEN_TILE = 512
FFN_COL_CHUNK = 256
SCAN_CHUNK = 128
SCAN_BLOCK = 512
CONV_HALO = BF16_SUBLANES


def _vmem_limit(estimate_bytes):
    return int(min(max(2 * estimate_bytes, 32 * 1024 * 1024), V7X_VMEM_BYTES - 8 * 1024 * 1024))


def _resident(shape):
    zeros = (0,) * len(shape)
    return pl.BlockSpec(shape, lambda *_: zeros, pipeline_mode=pl.Buffered(1))


def _rmsnorm(x, g):
    ms = jnp.mean(x * x, axis=-1, keepdims=True)
    return x * lax.rsqrt(ms + EPS) * g


def _dot(a, b):
    return jnp.dot(a, b, preferred_element_type=F32)


def _ffn_body(*refs, final_norm):
    if final_norm:
        x_ref, gn_ref, wg_ref, wu_ref, wd_ref, gf_ref, o_ref, h_scr = refs
    else:
        x_ref, gn_ref, wg_ref, wu_ref, wd_ref, o_ref, h_scr = refs
    x = x_ref[...]
    xn = _rmsnorm(x, gn_ref[...]).astype(BF16)
    d_ff = wg_ref.shape[1]
    for c in range(d_ff // FFN_COL_CHUNK):
        sl = slice(c * FFN_COL_CHUNK, (c + 1) * FFN_COL_CHUNK)
        gate = _dot(xn, wg_ref[:, sl])
        up = _dot(xn, wu_ref[:, sl])
        h_scr[:, sl] = (gate * jax.nn.sigmoid(gate) * up).astype(BF16)
    out = x + 0.5 * _dot(h_scr[...], wd_ref[...])
    if final_norm:
        out = _rmsnorm(out, gf_ref[...])
    o_ref[...] = out


def _ffn(x2d, g_norm, w_gate, w_up, w_down, g_final=None):
    t, d = x2d.shape
    d_ff = w_gate.shape[1]
    tm = TOKEN_TILE
    assert t % tm == 0 and d_ff % FFN_COL_CHUNK == 0
    final_norm = g_final is not None
    row = pl.BlockSpec((tm, d), lambda i: (i, 0))
    in_specs = [row, _resident((1, d)), _resident((d, d_ff)), _resident((d, d_ff)), _resident((d_ff, d))]
    args = [x2d, g_norm.reshape(1, d), w_gate, w_up, w_down]
    if final_norm:
        in_specs.append(_resident((1, d)))
        args.append(g_final.reshape(1, d))
    est = 3 * d * d_ff * 2 + 4 * tm * d * 4 + tm * d_ff * 2 + 4 * tm * d * 4
    return pl.pallas_call(
        functools.partial(_ffn_body, final_norm=final_norm),
        grid=(t // tm,),
        in_specs=in_specs,
        out_specs=row,
        out_shape=jax.ShapeDtypeStruct((t, d), F32),
        scratch_shapes=[pltpu.VMEM((tm, d_ff), BF16)],
        compiler_params=pltpu.CompilerParams(
            dimension_semantics=("parallel",), vmem_limit_bytes=_vmem_limit(est)),
        name="ffn",
    )(*args)


def _conv_body(xm_ref, xp_ref, xn_ref, g_ref, win_ref, wdw_ref, wout_ref, o_ref, h_scr):
    i = pl.program_id(1)
    n = pl.num_programs(1)
    tm = xm_ref.shape[1]
    d = xm_ref.shape[2]
    halo = CONV_HALO
    g = g_ref[...]
    xm = xm_ref[0]
    h_scr[0:halo] = _rmsnorm(xp_ref[0], g).astype(BF16)
    h_scr[halo:halo + tm] = _rmsnorm(xm, g).astype(BF16)
    h_scr[halo + tm:] = _rmsnorm(xn_ref[0], g).astype(BF16)
    h = h_scr[...]
    z = _dot(h, win_ref[:, d:2 * d]) * _dot(h, win_ref[:, 2 * d:])
    row = lax.broadcasted_iota(jnp.int32, (tm + 2 * halo, 1), 0)
    inside = jnp.logical_and(jnp.logical_or(row >= halo, i > 0),
                             jnp.logical_or(row < halo + tm, i < n - 1))
    z = jnp.where(inside, z, 0.0)
    rows = tm + 2 * halo
    z_prev = pltpu.roll(z, 1, axis=0)[halo:halo + tm]
    z_next = pltpu.roll(z, rows - 1, axis=0)[halo:halo + tm]
    y = wdw_ref[0:1] * z_prev + wdw_ref[1:2] * z[halo:halo + tm] + wdw_ref[2:3] * z_next
    b_gate = _dot(h_scr[halo:halo + tm], win_ref[:, 0:d])
    o_ref[0] = xm + _dot((b_gate * y).astype(BF16), wout_ref[...])


def _conv_mixer(x, g_norm, w_in, w_dw, w_out):
    b, s, d = x.shape
    tm = TOKEN_TILE
    halo = CONV_HALO
    assert s % tm == 0 and tm % halo == 0
    per = tm // halo
    last = s // halo - 1
    main = pl.BlockSpec((1, tm, d), lambda bi, i: (bi, i, 0))
    prev = pl.BlockSpec((1, halo, d), lambda bi, i: (bi, jnp.maximum(i * per - 1, 0), 0))
    nxt = pl.BlockSpec((1, halo, d), lambda bi, i: (bi, jnp.minimum((i + 1) * per, last), 0))
    est = 4 * d * d * 2 + 4 * tm * d * 4 + 8 * (tm + 2 * halo) * d * 4
    return pl.pallas_call(
        _conv_body,
        grid=(b, s // tm),
        in_specs=[main, prev, nxt, _resident((1, d)), _resident((d, 3 * d)),
                  _resident((3, d)), _resident((d, d))],
        out_specs=main,
        out_shape=jax.ShapeDtypeStruct((b, s, d), F32),
        scratch_shapes=[pltpu.VMEM((tm + 2 * halo, d), BF16)],
        compiler_params=pltpu.CompilerParams(
            dimension_semantics=("parallel", "parallel"), vmem_limit_bytes=_vmem_limit(est)),
        name="conv_mixer",
    )(x, x, x, g_norm.reshape(1, d), w_in, w_dw, w_out)


def _mlstm_in_body(x_ref, g_ref, win_ref, wgate_ref, wgate_t_ref, bgate_ref, bgate_t_ref,
                   q_ref, k_ref, v_ref, o_ref, gfw_ref, gbw_ref, gtfw_ref, gtbw_ref):
    d_qk = q_ref.shape[2]
    d_v = v_ref.shape[2]
    h = _rmsnorm(x_ref[0], g_ref[...]).astype(BF16)
    q_ref[0] = _dot(h, win_ref[:, 0:d_qk]).astype(BF16)
    k_ref[0] = (_dot(h, win_ref[:, d_qk:2 * d_qk]) * (DH_QK ** -0.5)).astype(BF16)
    v_ref[0] = _dot(h, win_ref[:, 2 * d_qk:2 * d_qk + d_v]).astype(BF16)
    o_ref[0] = _dot(h, win_ref[:, 2 * d_qk + d_v:])
    gates = _dot(h, wgate_ref[...]) + bgate_ref[...]
    half = 2 * N_HEADS
    gfw_ref[0] = gates[:, 0:half]
    gbw_ref[0] = gates[:, half:]
    gates_t = lax.dot_general(wgate_t_ref[...], h, (((1,), (1,)), ((), ())),
                              preferred_element_type=F32) + bgate_t_ref[...]
    for c in range(gtfw_ref.shape[1]):
        sl = slice(c * SCAN_CHUNK, (c + 1) * SCAN_CHUNK)
        gtfw_ref[0, c] = gates_t[0:half, sl]
        gtbw_ref[0, c] = gates_t[half:, sl]


def _mlstm_in(x, g_norm, w_in, w_gate, b_gate):
    b, s, d = x.shape
    tm = TOKEN_TILE
    d_qk = N_HEADS * DH_QK
    d_v = N_HEADS * DH_V
    n_gate = 4 * N_HEADS
    half = 2 * N_HEADS
    cpt = tm // SCAN_CHUNK
    assert s % tm == 0 and tm % SCAN_CHUNK == 0
    tok = lambda width: pl.BlockSpec((1, tm, width), lambda bi, i: (bi, i, 0))
    gt_spec = pl.BlockSpec((1, cpt, half, SCAN_CHUNK), lambda bi, i: (bi, i, 0, 0))
    gt_shape = jax.ShapeDtypeStruct((b, s // SCAN_CHUNK, half, SCAN_CHUNK), F32)
    est = d * (2 * d_qk + 2 * d_v) * 2 + 2 * tm * (d * 4 + (2 * d_qk + d_v) * 2 + d_v * 4) + 4 * tm * d * 4
    return pl.pallas_call(
        _mlstm_in_body,
        grid=(b, s // tm),
        in_specs=[tok(d), _resident((1, d)), _resident((d, 2 * d_qk + 2 * d_v)),
                  _resident((d, n_gate)), _resident((n_gate, d)),
                  _resident((1, n_gate)), _resident((n_gate, 1))],
        out_specs=[tok(d_qk), tok(d_qk), tok(d_v), tok(d_v), tok(half), tok(half), gt_spec, gt_spec],
        out_shape=[jax.ShapeDtypeStruct((b, s, d_qk), BF16), jax.ShapeDtypeStruct((b, s, d_qk), BF16),
                   jax.ShapeDtypeStruct((b, s, d_v), BF16), jax.ShapeDtypeStruct((b, s, d_v), F32),
                   jax.ShapeDtypeStruct((b, s, half), F32), jax.ShapeDtypeStruct((b, s, half), F32),
                   gt_shape, gt_shape],
        compiler_params=pltpu.CompilerParams(
            dimension_semantics=("parallel", "parallel"), vmem_limit_bytes=_vmem_limit(est)),
        name="mlstm_in",
    )(x, g_norm.reshape(1, d), w_in, w_gate, w_gate.T, b_gate.reshape(1, n_gate),
      b_gate.reshape(n_gate, 1))


def _log_sigmoid(x):
    return jnp.minimum(x, 0.0) - jnp.log1p(jnp.exp(-jnp.abs(x)))


def _scan_chunk(forward, q, k, v, g_col, g_row, c_ref, m_ref, h_ref, row0):
    nh = N_HEADS
    length = q.shape[0]
    ti = lax.broadcasted_iota(jnp.int32, (length, length), 0)
    ji = lax.broadcasted_iota(jnp.int32, (length, length), 1)
    lower = ji <= ti
    upper = ji >= ti
    seen = lower if forward else upper
    cum_col = seen.astype(F32)
    cum_row = (upper if forward else lower).astype(F32)
    hi = lax.Precision.HIGHEST

    i_col = g_col[:, 0:nh]
    logf_col = _log_sigmoid(g_col[:, nh:])
    i_row = g_row[0:nh, :]
    logf_row = _log_sigmoid(g_row[nh:, :])
    b_col = jnp.dot(cum_col, logf_col, precision=hi, preferred_element_type=F32)
    b_row = jnp.dot(logf_row, cum_row, precision=hi, preferred_element_type=F32)
    r_col = i_col - b_col
    r_row = i_row - b_row
    g_tot = jnp.sum(logf_col, axis=0, keepdims=True)
    a_col = g_tot + r_col
    m_loc = jnp.max(a_col, axis=0, keepdims=True)
    w_loc = jnp.exp(a_col - m_loc)
    m_prev = m_ref[...]
    m_new = jnp.maximum(g_tot + m_prev, m_loc)
    s_old = jnp.exp(g_tot + m_prev - m_new)
    s_loc = jnp.exp(m_loc - m_new)
    m_inter = b_col + m_prev

    lane = lax.broadcasted_iota(jnp.int32, (length, 2 * DH_QK), 1)
    ones_blk = jnp.ones((length, DH_V), BF16)
    state_row = lax.broadcasted_iota(jnp.int32, (2 * DH_QK, 1), 0)
    nt = (((1,), (1,)), ((), ()))
    tn = (((0,), (0,)), ((), ()))

    for p in range(nh // 2):
        qp = q[:, 2 * DH_QK * p:2 * DH_QK * (p + 1)]
        kp = k[:, 2 * DH_QK * p:2 * DH_QK * (p + 1)]
        c_st = c_ref[p]
        c_bf = c_st.astype(BF16)
        wvs = []
        for x in range(2):
            hd = 2 * p + x
            in_head = (lane < DH_QK) if x == 0 else (lane >= DH_QK)
            qm = jnp.where(in_head, qp, jnp.zeros_like(qp))
            s = lax.dot_general(qm, kp, nt, preferred_element_type=F32)
            d = jnp.where(seen, b_col[:, hd:hd + 1] + r_row[hd:hd + 1, :], -jnp.inf)
            mi = m_inter[:, hd:hd + 1]
            mo = jnp.maximum(mi, jnp.max(d, axis=1, keepdims=True))
            pmat = (s * jnp.exp(d - mo)).astype(BF16)
            v_aug = jnp.concatenate([v[:, DH_V * hd:DH_V * (hd + 1)], ones_blk], axis=1)
            nd = _dot(pmat, v_aug) + jnp.exp(mi - mo) * _dot(qm, c_bf)
            den = jnp.maximum(jnp.abs(nd[:, DH_V:]), jnp.exp(-mo))
            h_ref[0, pl.ds(row0, length), DH_V * hd:DH_V * (hd + 1)] = nd[:, 0:DH_V] / den
            wvs.append((w_loc[:, hd:hd + 1] * v_aug.astype(F32)).astype(BF16))
        wv = jnp.concatenate(wvs, axis=1)
        c_loc2 = lax.dot_general(kp, wv, tn, preferred_element_type=F32)
        c_loc = jnp.concatenate([c_loc2[0:DH_QK, 0:2 * DH_V], c_loc2[DH_QK:, 2 * DH_V:]], axis=0)
        first = state_row < DH_QK
        so = jnp.where(first, s_old[:, 2 * p:2 * p + 1], s_old[:, 2 * p + 1:2 * p + 2])
        sl = jnp.where(first, s_loc[:, 2 * p:2 * p + 1], s_loc[:, 2 * p + 1:2 * p + 2])
        c_ref[p] = so * c_st + sl * c_loc
    m_ref[...] = m_new


def _mlstm_scan_body(qf_ref, kf_ref, vf_ref, gf_ref, gtf_ref, qb_ref, kb_ref, vb_ref, gb_ref, gtb_ref,
                     hf_ref, hb_ref, cf_scr, cb_scr, mf_scr, mb_scr):
    @pl.when(pl.program_id(1) == 0)
    def _():
        cf_scr[...] = jnp.zeros_like(cf_scr)
        cb_scr[...] = jnp.zeros_like(cb_scr)
        mf_scr[...] = jnp.zeros_like(mf_scr)
        mb_scr[...] = jnp.zeros_like(mb_scr)

    length = SCAN_CHUNK
    n_chunks = qf_ref.shape[1] // length

    def body(c, carry):
        rf = pl.multiple_of(c * length, length)
        _scan_chunk(True, qf_ref[0, pl.ds(rf, length), :], kf_ref[0, pl.ds(rf, length), :],
                    vf_ref[0, pl.ds(rf, length), :], gf_ref[0, pl.ds(rf, length), :], gtf_ref[0, c],
                    cf_scr, mf_scr, hf_ref, rf)
        cb = n_chunks - 1 - c
        rb = pl.multiple_of(cb * length, length)
        _scan_chunk(False, qb_ref[0, pl.ds(rb, length), :], kb_ref[0, pl.ds(rb, length), :],
                    vb_ref[0, pl.ds(rb, length), :], gb_ref[0, pl.ds(rb, length), :], gtb_ref[0, cb],
                    cb_scr, mb_scr, hb_ref, rb)
        return carry

    lax.fori_loop(0, n_chunks, body, 0)


def _mlstm_scan(q, k, v, g_fw, g_bw, gt_fw, gt_bw):
    b, s, d_qk = q.shape
    d_v = v.shape[2]
    tb = SCAN_BLOCK
    nb = s // tb
    cpb = tb // SCAN_CHUNK
    half = 2 * N_HEADS
    assert s % tb == 0 and tb % SCAN_CHUNK == 0
    fw = lambda width: pl.BlockSpec((1, tb, width), lambda bi, i: (bi, i, 0))
    bw = lambda width: pl.BlockSpec((1, tb, width), lambda bi, i: (bi, nb - 1 - i, 0))
    gt_f = pl.BlockSpec((1, cpb, half, SCAN_CHUNK), lambda bi, i: (bi, i, 0, 0))
    gt_b = pl.BlockSpec((1, cpb, half, SCAN_CHUNK), lambda bi, i: (bi, nb - 1 - i, 0, 0))
    state = pltpu.VMEM((N_HEADS // 2, 2 * DH_QK, 2 * DH_V), F32)
    stab = pltpu.VMEM((1, N_HEADS), F32)
    est = 2 * 2 * tb * ((2 * d_qk + d_v) * 2 + d_v * 4 + 2 * half * 4) + 16 * SCAN_CHUNK * 1024 * 4
    return pl.pallas_call(
        _mlstm_scan_body,
        grid=(b, nb),
        in_specs=[fw(d_qk), fw(d_qk), fw(d_v), fw(half), gt_f,
                  bw(d_qk), bw(d_qk), bw(d_v), bw(half), gt_b],
        out_specs=[fw(d_v), bw(d_v)],
        out_shape=[jax.ShapeDtypeStruct((b, s, d_v), F32), jax.ShapeDtypeStruct((b, s, d_v), F32)],
        scratch_shapes=[state, state, stab, stab],
        compiler_params=pltpu.CompilerParams(
            dimension_semantics=("arbitrary", "arbitrary"), vmem_limit_bytes=_vmem_limit(est)),
        name="mlstm_scan",
    )(q, k, v, g_fw, gt_fw, q, k, v, g_bw, gt_bw)


def _mlstm_out_body(hf_ref, hb_ref, o_ref, x_ref, gh_ref, wout_ref, y_ref):
    h = hf_ref[...] + hb_ref[...]
    parts = []
    for hd in range(N_HEADS):
        hh = h[:, DH_V * hd:DH_V * (hd + 1)]
        parts.append(hh * lax.rsqrt(jnp.mean(hh * hh, axis=-1, keepdims=True) + EPS))
    hn = jnp.concatenate(parts, axis=1) * gh_ref[...]
    gated = (hn * jax.nn.sigmoid(o_ref[...])).astype(BF16)
    y_ref[...] = x_ref[...] + _dot(gated, wout_ref[...])


def _mlstm_out(h_fw, h_bw, o, x2d, g_head, w_out):
    t, d = x2d.shape
    d_v = h_fw.shape[1]
    tm = TOKEN_TILE
    assert t % tm == 0
    row_v = pl.BlockSpec((tm, d_v), lambda i: (i, 0))
    row_d = pl.BlockSpec((tm, d), lambda i: (i, 0))
    est = d_v * d * 2 + 2 * tm * (3 * d_v + 2 * d) * 4 + 4 * tm * d_v * 4
    return pl.pallas_call(
        _mlstm_out_body,
        grid=(t // tm,),
        in_specs=[row_v, row_v, row_v, row_d, _resident((1, d_v)), _resident((d_v, d))],
        out_specs=row_d,
        out_shape=jax.ShapeDtypeStruct((t, d), F32),
        compiler_params=pltpu.CompilerParams(
            dimension_semantics=("parallel",), vmem_limit_bytes=_vmem_limit(est)),
        name="mlstm_out",
    )(h_fw, h_bw, o, x2d, g_head.reshape(1, d_v), w_out)


def _trunk(x, p):
    b, s, d = x.shape
    flat = lambda a: a.reshape(b * s, a.shape[-1])
    depth = p["g_ffn1"].shape[0]
    for layer in range(depth):
        j = layer // 2
        x = _ffn(flat(x), p["g_ffn1"][layer], p["w_ffn1_gate"][layer], p["w_ffn1_up"][layer],
                 p["w_ffn1_down"][layer]).reshape(b, s, d)
        if layer % 2 == 0:
            x = _conv_mixer(x, p["g_mix"][layer], p["w_conv_in"][j], p["w_conv_dw"][j], p["w_conv_out"][j])
        else:
            q, k, v, o, g_fw, g_bw, gt_fw, gt_bw = _mlstm_in(
                x, p["g_mix"][layer], p["w_mlstm_in"][j], p["w_mlstm_gate"][j], p["b_mlstm_gate"][j])
            h_fw, h_bw = _mlstm_scan(q, k, v, g_fw, g_bw, gt_fw, gt_bw)
            x = _mlstm_out(flat(h_fw), flat(h_bw), flat(o), flat(x), p["g_mlstm_head"][j],
                           p["w_mlstm_out"][j]).reshape(b, s, d)
        g_final = p["g_final"] if layer == depth - 1 else None
        x = _ffn(flat(x), p["g_ffn2"][layer], p["w_ffn2_gate"][layer], p["w_ffn2_up"][layer],
                 p["w_ffn2_down"][layer], g_final).reshape(b, s, d)
    return x


def kernel(x_prompt, x_sample, g_ffn1, w_ffn1_gate, w_ffn1_up, w_ffn1_down, g_mix, w_conv_in, w_conv_dw, w_conv_out, w_mlstm_in, w_mlstm_gate, b_mlstm_gate, g_mlstm_head, w_mlstm_out, g_ffn2, w_ffn2_gate, w_ffn2_up, w_ffn2_down, g_final):
    bf = lambda w: w.astype(BF16)
    p = dict(
        g_ffn1=g_ffn1, w_ffn1_gate=bf(w_ffn1_gate), w_ffn1_up=bf(w_ffn1_up), w_ffn1_down=bf(w_ffn1_down),
        g_mix=g_mix, w_conv_in=bf(w_conv_in), w_conv_dw=w_conv_dw, w_conv_out=bf(w_conv_out),
        w_mlstm_in=bf(w_mlstm_in), w_mlstm_gate=bf(w_mlstm_gate), b_mlstm_gate=b_mlstm_gate,
        g_mlstm_head=g_mlstm_head, w_mlstm_out=bf(w_mlstm_out),
        g_ffn2=g_ffn2, w_ffn2_gate=bf(w_ffn2_gate), w_ffn2_up=bf(w_ffn2_up), w_ffn2_down=bf(w_ffn2_down),
        g_final=g_final)
    return _trunk(x_prompt, p), _trunk(x_sample, p)
```

```python
import functools

import jax
import jax.numpy as jnp
from jax import lax
from jax.experimental import pallas as pl
from jax.experimental.pallas import tpu as pltpu

F32 = jnp.float32
BF16 = jnp.bfloat16

N_HEADS = 8
DH_QK = 64
DH_V = 128
EPS = 1e-6

V7X_VMEM_BYTES = 64 * 1024 * 1024
V7X_LANES = 128
BF16_SUBLANES = 16

TOKEN_TILE = 512
FFN_COL_CHUNK = 256
SCAN_CHUNK = 128
SCAN_BLOCK = 512
CONV_HALO = BF16_SUBLANES


def _vmem_limit(estimate_bytes):
    return int(min(max(2 * estimate_bytes, 32 * 1024 * 1024), V7X_VMEM_BYTES - 8 * 1024 * 1024))


def _resident(shape):
    zeros = (0,) * len(shape)
    return pl.BlockSpec(shape, lambda *_: zeros, pipeline_mode=pl.Buffered(1))


def _rmsnorm(x, g):
    ms = jnp.mean(x * x, axis=-1, keepdims=True)
    return x * lax.rsqrt(ms + EPS) * g


def _dot(a, b):
    return jnp.dot(a, b, preferred_element_type=F32)


def _ffn_body(*refs, final_norm):
    if final_norm:
        x_ref, gn_ref, wg_ref, wu_ref, wd_ref, gf_ref, o_ref, h_scr = refs
    else:
        x_ref, gn_ref, wg_ref, wu_ref, wd_ref, o_ref, h_scr = refs
    x = x_ref[...]
    xn = _rmsnorm(x, gn_ref[...]).astype(BF16)
    d_ff = wg_ref.shape[1]
    for c in range(d_ff // FFN_COL_CHUNK):
        sl = slice(c * FFN_COL_CHUNK, (c + 1) * FFN_COL_CHUNK)
        gate = _dot(xn, wg_ref[:, sl])
        up = _dot(xn, wu_ref[:, sl])
        h_scr[:, sl] = (gate * jax.nn.sigmoid(gate) * up).astype(BF16)
    out = x + 0.5 * _dot(h_scr[...], wd_ref[...])
    if final_norm:
        out = _rmsnorm(out, gf_ref[...])
    o_ref[...] = out


def _ffn(x2d, g_norm, w_gate, w_up, w_down, g_final=None):
    t, d = x2d.shape
    d_ff = w_gate.shape[1]
    tm = TOKEN_TILE
    assert t % tm == 0 and d_ff % FFN_COL_CHUNK == 0
    final_norm = g_final is not None
    row = pl.BlockSpec((tm, d), lambda i: (i, 0))
    in_specs = [row, _resident((1, d)), _resident((d, d_ff)), _resident((d, d_ff)), _resident((d_ff, d))]
    args = [x2d, g_norm.reshape(1, d), w_gate, w_up, w_down]
    if final_norm:
        in_specs.append(_resident((1, d)))
        args.append(g_final.reshape(1, d))
    est = 3 * d * d_ff * 2 + 4 * tm * d * 4 + tm * d_ff * 2 + 4 * tm * d * 4
    return pl.pallas_call(
        functools.partial(_ffn_body, final_norm=final_norm),
        grid=(t // tm,),
        in_specs=in_specs,
        out_specs=row,
        out_shape=jax.ShapeDtypeStruct((t, d), F32),
        scratch_shapes=[pltpu.VMEM((tm, d_ff), BF16)],
        compiler_params=pltpu.CompilerParams(
            dimension_semantics=("parallel",), vmem_limit_bytes=_vmem_limit(est)),
        name="ffn",
    )(*args)


def _conv_body(xm_ref, xp_ref, xn_ref, g_ref, win_ref, wdw_ref, wout_ref, o_ref, h_scr):
    i = pl.program_id(1)
    n = pl.num_programs(1)
    tm = xm_ref.shape[1]
    d = xm_ref.shape[2]
    halo = CONV_HALO
    g = g_ref[...]
    xm = xm_ref[0]
    h_scr[0:halo] = _rmsnorm(xp_ref[0], g).astype(BF16)
    h_scr[halo:halo + tm] = _rmsnorm(xm, g).astype(BF16)
    h_scr[halo + tm:] = _rmsnorm(xn_ref[0], g).astype(BF16)
    h = h_scr[...]
    z = _dot(h, win_ref[:, d:2 * d]) * _dot(h, win_ref[:, 2 * d:])
    row = lax.broadcasted_iota(jnp.int32, (tm + 2 * halo, 1), 0)
    inside = jnp.logical_and(jnp.logical_or(row >= halo, i > 0),
                             jnp.logical_or(row < halo + tm, i < n - 1))
    z = jnp.where(inside, z, 0.0)
    rows = tm + 2 * halo
    z_prev = pltpu.roll(z, 1, axis=0)[halo:halo + tm]
    z_next = pltpu.roll(z, rows - 1, axis=0)[halo:halo + tm]
    y = wdw_ref[0:1] * z_prev + wdw_ref[1:2] * z[halo:halo + tm] + wdw_ref[2:3] * z_next
    b_gate = _dot(h_scr[halo:halo + tm], win_ref[:, 0:d])
    o_ref[0] = xm + _dot((b_gate * y).astype(BF16), wout_ref[...])


def _conv_mixer(x, g_norm, w_in, w_dw, w_out):
    b, s, d = x.shape
    tm = TOKEN_TILE
    halo = CONV_HALO
    assert s % tm == 0 and tm % halo == 0
    per = tm // halo
    last = s // halo - 1
    main = pl.BlockSpec((1, tm, d), lambda bi, i: (bi, i, 0))
    prev = pl.BlockSpec((1, halo, d), lambda bi, i: (bi, jnp.maximum(i * per - 1, 0), 0))
    nxt = pl.BlockSpec((1, halo, d), lambda bi, i: (bi, jnp.minimum((i + 1) * per, last), 0))
    est = 4 * d * d * 2 + 4 * tm * d * 4 + 8 * (tm + 2 * halo) * d * 4
    return pl.pallas_call(
        _conv_body,
        grid=(b, s // tm),
        in_specs=[main, prev, nxt, _resident((1, d)), _resident((d, 3 * d)),
                  _resident((3, d)), _resident((d, d))],
        out_specs=main,
        out_shape=jax.ShapeDtypeStruct((b, s, d), F32),
        scratch_shapes=[pltpu.VMEM((tm + 2 * halo, d), BF16)],
        compiler_params=pltpu.CompilerParams(
            dimension_semantics=("parallel", "parallel"), vmem_limit_bytes=_vmem_limit(est)),
        name="conv_mixer",
    )(x, x, x, g_norm.reshape(1, d), w_in, w_dw, w_out)


def _log_sigmoid(x):
    return jnp.minimum(x, 0.0) - jnp.log1p(jnp.exp(-jnp.abs(x)))


def _lane_scan(x, op, identity, forward):
    length = x.shape[1]
    lane = lax.broadcasted_iota(jnp.int32, x.shape, 1)
    shift = 1
    while shift < length:
        if forward:
            moved = pltpu.roll(x, shift, axis=1)
            valid = lane >= shift
        else:
            moved = pltpu.roll(x, length - shift, axis=1)
            valid = lane < length - shift
        x = op(x, jnp.where(valid, moved, identity))
        shift *= 2
    return x


def _bf16_pieces(x, n):
    pieces = []
    for _ in range(n):
        piece = x.astype(BF16).astype(F32)
        pieces.append(piece)
        x = x - piece
    return pieces


def _chunk_gate_features(gates, forward):
    nh = N_HEADS
    length = gates.shape[1]
    logf = _log_sigmoid(gates[nh:, :])
    b = _lane_scan(logf, jnp.add, 0.0, forward)
    r = gates[0:nh, :] - b
    cm = _lane_scan(r, jnp.maximum, -jnp.inf, forward)
    g_tot = jnp.broadcast_to(jnp.sum(logf, axis=1, keepdims=True), (nh, length))
    a = g_tot + r
    m_loc = jnp.broadcast_to(jnp.max(a, axis=1, keepdims=True), (nh, length))
    w_loc = jnp.exp(a - m_loc)
    pieces = _bf16_pieces(-cm, 3) + _bf16_pieces(-b, 3) + _bf16_pieces(w_loc, 2)
    stacked = jnp.concatenate(pieces, axis=0).astype(BF16)
    return jnp.concatenate([r, g_tot, m_loc], axis=0), stacked


def _pieces_to_cols(stacked):
    nh = N_HEADS
    n_rows = stacked.shape[0]
    prow = lax.broadcasted_iota(jnp.int32, (n_rows, 4 * nh), 0)
    pcol = lax.broadcasted_iota(jnp.int32, (n_rows, 4 * nh), 1)
    quantity = (prow >= 3 * nh).astype(jnp.int32) + (prow >= 6 * nh).astype(jnp.int32)
    gather = (pcol == (prow % nh) + nh * quantity).astype(BF16)
    return lax.dot_general(stacked, gather, (((0,), (0,)), ((), ())), preferred_element_type=F32)


def _mlstm_in_body(x_ref, g_ref, win_ref, wgate_t_ref, bgate_t_ref,
                   q_ref, k_ref, v_ref, o_ref, rows_fw_ref, rows_bw_ref, cols_fw_ref, cols_bw_ref):
    d_qk = q_ref.shape[2]
    d_v = v_ref.shape[2]
    h = _rmsnorm(x_ref[0], g_ref[...]).astype(BF16)
    gates_t = lax.dot_general(wgate_t_ref[...], h, (((1,), (1,)), ((), ())),
                              preferred_element_type=F32) + bgate_t_ref[...]
    half = 2 * N_HEADS
    stacked = []
    for c in range(rows_fw_ref.shape[1]):
        sl = slice(c * SCAN_CHUNK, (c + 1) * SCAN_CHUNK)
        rows_fw_ref[0, c], st_fw = _chunk_gate_features(gates_t[0:half, sl], True)
        rows_bw_ref[0, c], st_bw = _chunk_gate_features(gates_t[half:, sl], False)
        stacked.append((sl, st_fw, st_bw))
    q_ref[0] = _dot(h, win_ref[:, 0:d_qk]).astype(BF16)
    k_ref[0] = (_dot(h, win_ref[:, d_qk:2 * d_qk]) * (DH_QK ** -0.5)).astype(BF16)
    v_ref[0] = _dot(h, win_ref[:, 2 * d_qk:2 * d_qk + d_v]).astype(BF16)
    o_ref[0] = _dot(h, win_ref[:, 2 * d_qk + d_v:])
    for sl, st_fw, st_bw in stacked:
        cols_fw_ref[0, sl, :] = _pieces_to_cols(st_fw)
        cols_bw_ref[0, sl, :] = _pieces_to_cols(st_bw)


def _mlstm_in(x, g_norm, w_in, w_gate, b_gate):
    b, s, d = x.shape
    tm = TOKEN_TILE
    d_qk = N_HEADS * DH_QK
    d_v = N_HEADS * DH_V
    n_gate = 4 * N_HEADS
    n_rows = 3 * N_HEADS
    cpt = tm // SCAN_CHUNK
    assert s % tm == 0 and tm % SCAN_CHUNK == 0
    tok = lambda width: pl.BlockSpec((1, tm, width), lambda bi, i: (bi, i, 0))
    rows_spec = pl.BlockSpec((1, cpt, n_rows, SCAN_CHUNK), lambda bi, i: (bi, i, 0, 0))
    rows_shape = jax.ShapeDtypeStruct((b, s // SCAN_CHUNK, n_rows, SCAN_CHUNK), F32)
    cols_shape = jax.ShapeDtypeStruct((b, s, n_gate), F32)
    est = d * (2 * d_qk + 2 * d_v) * 2 + 2 * tm * (d * 4 + (2 * d_qk + d_v) * 2 + d_v * 4) + 4 * tm * d * 4
    return pl.pallas_call(
        _mlstm_in_body,
        grid=(b, s // tm),
        in_specs=[tok(d), _resident((1, d)), _resident((d, 2 * d_qk + 2 * d_v)),
                  _resident((n_gate, d)), _resident((n_gate, 1))],
        out_specs=[tok(d_qk), tok(d_qk), tok(d_v), tok(d_v), rows_spec, rows_spec, tok(n_gate), tok(n_gate)],
        out_shape=[jax.ShapeDtypeStruct((b, s, d_qk), BF16), jax.ShapeDtypeStruct((b, s, d_qk), BF16),
                   jax.ShapeDtypeStruct((b, s, d_v), BF16), jax.ShapeDtypeStruct((b, s, d_v), F32),
                   rows_shape, rows_shape, cols_shape, cols_shape],
        compiler_params=pltpu.CompilerParams(
            dimension_semantics=("parallel", "parallel"), vmem_limit_bytes=_vmem_limit(est)),
        name="mlstm_in",
    )(x, g_norm.reshape(1, d), w_in, w_gate.T, b_gate.reshape(n_gate, 1))


def _scan_chunk(forward, q, k, v, rows, cols, c_ref, m_ref, h_ref, row0):
    nh = N_HEADS
    length = q.shape[0]
    ti = lax.broadcasted_iota(jnp.int32, (length, length), 0)
    ji = lax.broadcasted_iota(jnp.int32, (length, length), 1)
    seen = (ji <= ti) if forward else (ji >= ti)

    r = rows[0:nh, :]
    g_tot = rows[nh:2 * nh, :]
    m_loc = rows[2 * nh:, :]
    m_prev = m_ref[...]
    m_new = jnp.maximum(g_tot + m_prev, m_loc)
    s_old = jnp.exp(g_tot + m_prev - m_new)
    s_loc = jnp.exp(m_loc - m_new)
    r_rel = r - m_prev
    tn = (((0,), (0,)), ((), ()))
    nt = (((1,), (1,)), ((), ()))

    tile = (length, 2 * DH_QK)
    low = lax.broadcasted_iota(jnp.int32, tile, 1) < DH_QK
    ones_blk = jnp.ones((length, DH_V), BF16)
    eye = (ti == ji).astype(BF16)

    for p in range(nh // 2):
        qp = q[:, 2 * DH_QK * p:2 * DH_QK * (p + 1)]
        kp = k[:, 2 * DH_QK * p:2 * DH_QK * (p + 1)]
        k_eye = jnp.concatenate([kp, eye], axis=0)
        c_st = c_ref[p]
        c_bf = c_st.astype(BF16)
        v_augs = []
        for x in range(2):
            hd = 2 * p + x
            qm = jnp.where(low if x == 0 else jnp.logical_not(low), qp, jnp.zeros_like(qp))
            s_ext = lax.dot_general(qm, k_eye, nt, preferred_element_type=F32)
            neg_cm = jnp.broadcast_to(cols[:, hd:hd + 1], tile)
            mp = m_prev[hd:hd + 1, :]
            arg = jnp.concatenate(
                [jnp.where(seen, jnp.minimum(neg_cm + r[hd:hd + 1, :], r_rel[hd:hd + 1, :]), -jnp.inf),
                 jnp.minimum(neg_cm + mp, 0.0)], axis=1)
            p_ext = (s_ext * jnp.exp(arg)).astype(BF16)
            v_aug = jnp.concatenate([v[:, DH_V * hd:DH_V * (hd + 1)], ones_blk], axis=1)
            v_augs.append(v_aug)
            nd = _dot(p_ext, jnp.concatenate([v_aug, c_bf], axis=0))
            neg_b = jnp.broadcast_to(cols[:, nh + hd:nh + hd + 1], tile)
            floor = jnp.exp(neg_b + jnp.minimum(neg_cm, -mp))
            den = jnp.maximum(jnp.abs(nd[:, DH_V:]), floor)
            h_ref[0, pl.ds(row0, length), DH_V * hd:DH_V * (hd + 1)] = nd[:, 0:DH_V] / den
        kf = kp.astype(F32)
        w0 = jnp.broadcast_to(cols[:, 2 * nh + 2 * p:2 * nh + 2 * p + 1], tile)
        w1 = jnp.broadcast_to(cols[:, 2 * nh + 2 * p + 1:2 * nh + 2 * p + 2], tile)
        kw = jnp.concatenate([kf * jnp.where(low, w0, 0.0), kf * jnp.where(low, 0.0, w1)], axis=0)
        c_loc = lax.dot_general(kw.astype(BF16), jnp.concatenate(v_augs, axis=0), tn,
                                preferred_element_type=F32)
        per_head = (DH_QK, length)
        so = jnp.concatenate([jnp.broadcast_to(s_old[2 * p:2 * p + 1, :], per_head),
                              jnp.broadcast_to(s_old[2 * p + 1:2 * p + 2, :], per_head)], axis=0)
        sl = jnp.concatenate([jnp.broadcast_to(s_loc[2 * p:2 * p + 1, :], per_head),
                              jnp.broadcast_to(s_loc[2 * p + 1:2 * p + 2, :], per_head)], axis=0)
        c_ref[p] = jnp.concatenate([so, so], axis=1) * c_st + jnp.concatenate([sl, sl], axis=1) * c_loc
    m_ref[...] = m_new


def _mlstm_scan_body(qf_ref, kf_ref, vf_ref, rowsf_ref, colsf_ref, qb_ref, kb_ref, vb_ref, rowsb_ref, colsb_ref,
                     hf_ref, hb_ref, cf_scr, cb_scr, mf_scr, mb_scr):
    @pl.when(pl.program_id(1) == 0)
    def _():
        cf_scr[...] = jnp.zeros_like(cf_scr)
        cb_scr[...] = jnp.zeros_like(cb_scr)
        mf_scr[...] = jnp.zeros_like(mf_scr)
        mb_scr[...] = jnp.zeros_like(mb_scr)

    length = SCAN_CHUNK
    n_chunks = qf_ref.shape[1] // length

    def body(c, carry):
        rf = pl.multiple_of(c * length, length)
        _scan_chunk(True, qf_ref[0, pl.ds(rf, length), :], kf_ref[0, pl.ds(rf, length), :],
                    vf_ref[0, pl.ds(rf, length), :], rowsf_ref[0, c], colsf_ref[0, pl.ds(rf, length), :],
                    cf_scr, mf_scr, hf_ref, rf)
        cb = n_chunks - 1 - c
        rb = pl.multiple_of(cb * length, length)
        _scan_chunk(False, qb_ref[0, pl.ds(rb, length), :], kb_ref[0, pl.ds(rb, length), :],
                    vb_ref[0, pl.ds(rb, length), :], rowsb_ref[0, cb], colsb_ref[0, pl.ds(rb, length), :],
                    cb_scr, mb_scr, hb_ref, rb)
        return carry

    lax.fori_loop(0, n_chunks, body, 0)


def _mlstm_scan(q, k, v, rows_fw, rows_bw, cols_fw, cols_bw):
    b, s, d_qk = q.shape
    d_v = v.shape[2]
    tb = SCAN_BLOCK
    nb = s // tb
    cpb = tb // SCAN_CHUNK
    n_rows = rows_fw.shape[2]
    n_cols = cols_fw.shape[2]
    assert s % tb == 0 and tb % SCAN_CHUNK == 0
    assert SCAN_CHUNK == V7X_LANES and 2 * DH_QK == V7X_LANES and DH_V == V7X_LANES
    fw = lambda width: pl.BlockSpec((1, tb, width), lambda bi, i: (bi, i, 0))
    bw = lambda width: pl.BlockSpec((1, tb, width), lambda bi, i: (bi, nb - 1 - i, 0))
    rows_f = pl.BlockSpec((1, cpb, n_rows, SCAN_CHUNK), lambda bi, i: (bi, i, 0, 0))
    rows_b = pl.BlockSpec((1, cpb, n_rows, SCAN_CHUNK), lambda bi, i: (bi, nb - 1 - i, 0, 0))
    state = pltpu.VMEM((N_HEADS // 2, 2 * DH_QK, 2 * DH_V), F32)
    stab = pltpu.VMEM((N_HEADS, SCAN_CHUNK), F32)
    est = 2 * 2 * tb * ((2 * d_qk + d_v) * 2 + d_v * 4 + (n_rows + V7X_LANES) * 4) + 16 * SCAN_CHUNK * 1024 * 4
    return pl.pallas_call(
        _mlstm_scan_body,
        grid=(b, nb),
        in_specs=[fw(d_qk), fw(d_qk), fw(d_v), rows_f, fw(n_cols),
                  bw(d_qk), bw(d_qk), bw(d_v), rows_b, bw(n_cols)],
        out_specs=[fw(d_v), bw(d_v)],
        out_shape=[jax.ShapeDtypeStruct((b, s, d_v), F32), jax.ShapeDtypeStruct((b, s, d_v), F32)],
        scratch_shapes=[state, state, stab, stab],
        compiler_params=pltpu.CompilerParams(
            dimension_semantics=("arbitrary", "arbitrary"), vmem_limit_bytes=_vmem_limit(est)),
        name="mlstm_scan",
    )(q, k, v, rows_fw, cols_fw, q, k, v, rows_bw, cols_bw)


def _mlstm_out_body(hf_ref, hb_ref, o_ref, x_ref, gh_ref, wout_ref, y_ref):
    h = hf_ref[...] + hb_ref[...]
    parts = []
    for hd in range(N_HEADS):
        hh = h[:, DH_V * hd:DH_V * (hd + 1)]
        parts.append(hh * lax.rsqrt(jnp.mean(hh * hh, axis=-1, keepdims=True) + EPS))
    hn = jnp.concatenate(parts, axis=1) * gh_ref[...]
    gated = (hn * jax.nn.sigmoid(o_ref[...])).astype(BF16)
    y_ref[...] = x_ref[...] + _dot(gated, wout_ref[...])


def _mlstm_out(h_fw, h_bw, o, x2d, g_head, w_out):
    t, d = x2d.shape
    d_v = h_fw.shape[1]
    tm = TOKEN_TILE
    assert t % tm == 0
    row_v = pl.BlockSpec((tm, d_v), lambda i: (i, 0))
    row_d = pl.BlockSpec((tm, d), lambda i: (i, 0))
    est = d_v * d * 2 + 2 * tm * (3 * d_v + 2 * d) * 4 + 4 * tm * d_v * 4
    return pl.pallas_call(
        _mlstm_out_body,
        grid=(t // tm,),
        in_specs=[row_v, row_v, row_v, row_d, _resident((1, d_v)), _resident((d_v, d))],
        out_specs=row_d,
        out_shape=jax.ShapeDtypeStruct((t, d), F32),
        compiler_params=pltpu.CompilerParams(
            dimension_semantics=("parallel",), vmem_limit_bytes=_vmem_limit(est)),
        name="mlstm_out",
    )(h_fw, h_bw, o, x2d, g_head.reshape(1, d_v), w_out)


def _trunk(x, p):
    b, s, d = x.shape
    flat = lambda a: a.reshape(b * s, a.shape[-1])
    depth = p["g_ffn1"].shape[0]
    for layer in range(depth):
        j = layer // 2
        x = _ffn(flat(x), p["g_ffn1"][layer], p["w_ffn1_gate"][layer], p["w_ffn1_up"][layer],
                 p["w_ffn1_down"][layer]).reshape(b, s, d)
        if layer % 2 == 0:
            x = _conv_mixer(x, p["g_mix"][layer], p["w_conv_in"][j], p["w_conv_dw"][j], p["w_conv_out"][j])
        else:
            q, k, v, o, rows_fw, rows_bw, cols_fw, cols_bw = _mlstm_in(
                x, p["g_mix"][layer], p["w_mlstm_in"][j], p["w_mlstm_gate"][j], p["b_mlstm_gate"][j])
            h_fw, h_bw = _mlstm_scan(q, k, v, rows_fw, rows_bw, cols_fw, cols_bw)
            x = _mlstm_out(flat(h_fw), flat(h_bw), flat(o), flat(x), p["g_mlstm_head"][j],
                           p["w_mlstm_out"][j]).reshape(b, s, d)
        g_final = p["g_final"] if layer == depth - 1 else None
        x = _ffn(flat(x), p["g_ffn2"][layer], p["w_ffn2_gate"][layer], p["w_ffn2_up"][layer],
                 p["w_ffn2_down"][layer], g_final).reshape(b, s, d)
    return x


def kernel(x_prompt, x_sample, g_ffn1, w_ffn1_gate, w_ffn1_up, w_ffn1_down, g_mix, w_conv_in, w_conv_dw, w_conv_out, w_mlstm_in, w_mlstm_gate, b_mlstm_gate, g_mlstm_head, w_mlstm_out, g_ffn2, w_ffn2_gate, w_ffn2_up, w_ffn2_down, g_final):
    bf = lambda w: w.astype(BF16)
    p = dict(
        g_ffn1=g_ffn1, w_ffn1_gate=bf(w_ffn1_gate), w_ffn1_up=bf(w_ffn1_up), w_ffn1_down=bf(w_ffn1_down),
        g_mix=g_mix, w_conv_in=bf(w_conv_in), w_conv_dw=w_conv_dw, w_conv_out=bf(w_conv_out),
        w_mlstm_in=bf(w_mlstm_in), w_mlstm_gate=bf(w_mlstm_gate), b_mlstm_gate=b_mlstm_gate,
        g_mlstm_head=g_mlstm_head, w_mlstm_out=bf(w_mlstm_out),
        g_ffn2=g_ffn2, w_ffn2_gate=bf(w_ffn2_gate), w_ffn2_up=bf(w_ffn2_up), w_ffn2_down=bf(w_ffn2_down),
        g_final=g_final)
    return _trunk(x_prompt, p), _trunk(x_sample, p)
```

```python
import functools

import jax
import jax.numpy as jnp
from jax import lax
from jax.experimental import pallas as pl
from jax.experimental.pallas import tpu as pltpu

F32 = jnp.float32
BF16 = jnp.bfloat16

N_HEADS = 8
DH_QK = 64
DH_V = 128
EPS = 1e-6
LOG2_E = 1.4426950408889634

V7X_VMEM_BYTES = 64 * 1024 * 1024
V7X_LANES = 128
BF16_SUBLANES = 16

TOKEN_TILE = 512
FFN_TILE = 1024
FFN_COL_CHUNK = 256
SCAN_CHUNK = 128
SCAN_BLOCK = 512
CONV_HALO = BF16_SUBLANES


def _vmem_limit(estimate_bytes):
    return int(min(max(2 * estimate_bytes, 32 * 1024 * 1024), V7X_VMEM_BYTES - 8 * 1024 * 1024))


def _resident(shape):
    zeros = (0,) * len(shape)
    return pl.BlockSpec(shape, lambda *_: zeros, pipeline_mode=pl.Buffered(1))


def _rmsnorm(x, g):
    ms = jnp.mean(x * x, axis=-1, keepdims=True)
    return x * lax.rsqrt(ms + EPS) * g


def _dot(a, b):
    return jnp.dot(a, b, preferred_element_type=F32)


def _mlstm_output(h, o, g_head, w_out):
    parts = []
    for hd in range(N_HEADS):
        hh = h[:, DH_V * hd:DH_V * (hd + 1)]
        parts.append(hh * lax.rsqrt(jnp.mean(hh * hh, axis=-1, keepdims=True) + EPS))
    hn = jnp.concatenate(parts, axis=1) * g_head
    return _dot((hn * jax.nn.sigmoid(o)).astype(BF16), w_out)


def _ffn_body(*refs, mlstm_pre, final_norm):
    refs = list(refs)
    x_ref, gn_ref, wg_ref, wu_ref, wd_ref = refs[:5]
    o_ref, h_scr = refs[-2:]
    extra = refs[5:-2]
    x = x_ref[...]
    if mlstm_pre:
        hf_ref, hb_ref, og_ref, gh_ref, wo_ref = extra[:5]
        x = x + _mlstm_output(hf_ref[...] + hb_ref[...], og_ref[...], gh_ref[...], wo_ref[...])
    if final_norm:
        gf_ref = extra[-1]
    xn = _rmsnorm(x, gn_ref[...]).astype(BF16)
    d_ff = wg_ref.shape[1]
    for c in range(d_ff // FFN_COL_CHUNK):
        sl = slice(c * FFN_COL_CHUNK, (c + 1) * FFN_COL_CHUNK)
        gate = _dot(xn, wg_ref[:, sl])
        up = _dot(xn, wu_ref[:, sl])
        h_scr[:, sl] = (gate * jax.nn.sigmoid(gate) * up).astype(BF16)
    out = x + 0.5 * _dot(h_scr[...], wd_ref[...])
    if final_norm:
        out = _rmsnorm(out, gf_ref[...])
    o_ref[...] = out


def _ffn(x2d, g_norm, w_gate, w_up, w_down, mlstm=None, g_final=None):
    t, d = x2d.shape
    d_ff = w_gate.shape[1]
    tm = TOKEN_TILE if mlstm is not None else FFN_TILE
    assert t % tm == 0 and d_ff % FFN_COL_CHUNK == 0
    final_norm = g_final is not None
    row = pl.BlockSpec((tm, d), lambda i: (i, 0))
    in_specs = [row, _resident((1, d)), _resident((d, d_ff)), _resident((d, d_ff)), _resident((d_ff, d))]
    args = [x2d, g_norm.reshape(1, d), w_gate, w_up, w_down]
    est = 3 * d * d_ff * 2 + 4 * tm * d * 4 + tm * d_ff * 2 + 4 * tm * d * 4
    if mlstm is not None:
        h_fw, h_bw, o, g_head, w_out = mlstm
        d_v = h_fw.shape[1]
        row_v = pl.BlockSpec((tm, d_v), lambda i: (i, 0))
        in_specs += [row_v, row_v, row_v, _resident((1, d_v)), _resident((d_v, d))]
        args += [h_fw, h_bw, o, g_head.reshape(1, d_v), w_out]
        est += d_v * d * 2 + 8 * tm * d_v * 4
    if final_norm:
        in_specs.append(_resident((1, d)))
        args.append(g_final.reshape(1, d))
    return pl.pallas_call(
        functools.partial(_ffn_body, mlstm_pre=mlstm is not None, final_norm=final_norm),
        grid=(t // tm,),
        in_specs=in_specs,
        out_specs=row,
        out_shape=jax.ShapeDtypeStruct((t, d), F32),
        scratch_shapes=[pltpu.VMEM((tm, d_ff), BF16)],
        compiler_params=pltpu.CompilerParams(
            dimension_semantics=("parallel",), vmem_limit_bytes=_vmem_limit(est)),
        name="ffn",
    )(*args)


def _conv_body(xm_ref, xp_ref, xn_ref, g_ref, win_ref, wdw_ref, wout_ref, o_ref, h_scr):
    i = pl.program_id(1)
    n = pl.num_programs(1)
    tm = xm_ref.shape[1]
    d = xm_ref.shape[2]
    halo = CONV_HALO
    g = g_ref[...]
    xm = xm_ref[0]
    h_scr[0:halo] = _rmsnorm(xp_ref[0], g).astype(BF16)
    h_scr[halo:halo + tm] = _rmsnorm(xm, g).astype(BF16)
    h_scr[halo + tm:] = _rmsnorm(xn_ref[0], g).astype(BF16)
    h = h_scr[...]
    z = _dot(h, win_ref[:, d:2 * d]) * _dot(h, win_ref[:, 2 * d:])
    row = lax.broadcasted_iota(jnp.int32, (tm + 2 * halo, 1), 0)
    inside = jnp.logical_and(jnp.logical_or(row >= halo, i > 0),
                             jnp.logical_or(row < halo + tm, i < n - 1))
    z = jnp.where(inside, z, 0.0)
    rows = tm + 2 * halo
    z_prev = pltpu.roll(z, 1, axis=0)[halo:halo + tm]
    z_next = pltpu.roll(z, rows - 1, axis=0)[halo:halo + tm]
    y = wdw_ref[0:1] * z_prev + wdw_ref[1:2] * z[halo:halo + tm] + wdw_ref[2:3] * z_next
    b_gate = _dot(h_scr[halo:halo + tm], win_ref[:, 0:d])
    o_ref[0] = xm + _dot((b_gate * y).astype(BF16), wout_ref[...])


def _conv_mixer(x, g_norm, w_in, w_dw, w_out):
    b, s, d = x.shape
    tm = TOKEN_TILE
    halo = CONV_HALO
    assert s % tm == 0 and tm % halo == 0
    per = tm // halo
    last = s // halo - 1
    main = pl.BlockSpec((1, tm, d), lambda bi, i: (bi, i, 0))
    prev = pl.BlockSpec((1, halo, d), lambda bi, i: (bi, jnp.maximum(i * per - 1, 0), 0))
    nxt = pl.BlockSpec((1, halo, d), lambda bi, i: (bi, jnp.minimum((i + 1) * per, last), 0))
    est = 4 * d * d * 2 + 4 * tm * d * 4 + 8 * (tm + 2 * halo) * d * 4
    return pl.pallas_call(
        _conv_body,
        grid=(b, s // tm),
        in_specs=[main, prev, nxt, _resident((1, d)), _resident((d, 3 * d)),
                  _resident((3, d)), _resident((d, d))],
        out_specs=main,
        out_shape=jax.ShapeDtypeStruct((b, s, d), F32),
        scratch_shapes=[pltpu.VMEM((tm + 2 * halo, d), BF16)],
        compiler_params=pltpu.CompilerParams(
            dimension_semantics=("parallel", "parallel"), vmem_limit_bytes=_vmem_limit(est)),
        name="conv_mixer",
    )(x, x, x, g_norm.reshape(1, d), w_in, w_dw, w_out)


def _log_sigmoid(x):
    return jnp.minimum(x, 0.0) - jnp.log1p(jnp.exp(-jnp.abs(x)))


def _lane_scan(x, op, identity, forward):
    length = x.shape[1]
    lane = lax.broadcasted_iota(jnp.int32, x.shape, 1)
    shift = 1
    while shift < length:
        if forward:
            moved = pltpu.roll(x, shift, axis=1)
            valid = lane >= shift
        else:
            moved = pltpu.roll(x, length - shift, axis=1)
            valid = lane < length - shift
        x = op(x, jnp.where(valid, moved, identity))
        shift *= 2
    return x


def _bf16_pieces(x, n):
    pieces = []
    for _ in range(n):
        piece = x.astype(BF16).astype(F32)
        pieces.append(piece)
        x = x - piece
    return pieces


def _chunk_gate_features(gates, forward):
    nh = N_HEADS
    length = gates.shape[1]
    logf = _log_sigmoid(gates[nh:, :]) * LOG2_E
    b = _lane_scan(logf, jnp.add, 0.0, forward)
    r = gates[0:nh, :] * LOG2_E - b
    cm = _lane_scan(r, jnp.maximum, -jnp.inf, forward)
    g_tot = jnp.broadcast_to(jnp.sum(logf, axis=1, keepdims=True), (nh, length))
    a = g_tot + r
    m_loc = jnp.broadcast_to(jnp.max(a, axis=1, keepdims=True), (nh, length))
    w_loc = jnp.exp2(a - m_loc)
    pieces = _bf16_pieces(-cm, 3) + _bf16_pieces(-b, 3) + _bf16_pieces(w_loc, 2)
    stacked = jnp.concatenate(pieces, axis=0).astype(BF16)
    return jnp.concatenate([r, g_tot, m_loc], axis=0), stacked


def _pieces_to_cols(stacked):
    nh = N_HEADS
    n_rows = stacked.shape[0]
    prow = lax.broadcasted_iota(jnp.int32, (n_rows, 4 * nh), 0)
    pcol = lax.broadcasted_iota(jnp.int32, (n_rows, 4 * nh), 1)
    quantity = (prow >= 3 * nh).astype(jnp.int32) + (prow >= 6 * nh).astype(jnp.int32)
    gather = (pcol == (prow % nh) + nh * quantity).astype(BF16)
    return lax.dot_general(stacked, gather, (((0,), (0,)), ((), ())), preferred_element_type=F32)


def _mlstm_in_body(x_ref, g_ref, win_ref, wgate_t_ref, bgate_t_ref,
                   q_ref, k_ref, v_ref, o_ref, rows_fw_ref, rows_bw_ref, cols_fw_ref, cols_bw_ref):
    d_qk = q_ref.shape[2]
    d_v = v_ref.shape[2]
    h = _rmsnorm(x_ref[0], g_ref[...]).astype(BF16)
    gates_t = lax.dot_general(wgate_t_ref[...], h, (((1,), (1,)), ((), ())),
                              preferred_element_type=F32) + bgate_t_ref[...]
    half = 2 * N_HEADS
    stacked = []
    for c in range(rows_fw_ref.shape[1]):
        sl = slice(c * SCAN_CHUNK, (c + 1) * SCAN_CHUNK)
        rows_fw_ref[0, c], st_fw = _chunk_gate_features(gates_t[0:half, sl], True)
        rows_bw_ref[0, c], st_bw = _chunk_gate_features(gates_t[half:, sl], False)
        stacked.append((sl, st_fw, st_bw))
    q_ref[0] = _dot(h, win_ref[:, 0:d_qk]).astype(BF16)
    k_ref[0] = (_dot(h, win_ref[:, d_qk:2 * d_qk]) * (DH_QK ** -0.5)).astype(BF16)
    v_ref[0] = _dot(h, win_ref[:, 2 * d_qk:2 * d_qk + d_v]).astype(BF16)
    o_ref[0] = _dot(h, win_ref[:, 2 * d_qk + d_v:])
    for sl, st_fw, st_bw in stacked:
        cols_fw_ref[0, sl, :] = _pieces_to_cols(st_fw)
        cols_bw_ref[0, sl, :] = _pieces_to_cols(st_bw)


def _mlstm_in(x, g_norm, w_in, w_gate, b_gate):
    b, s, d = x.shape
    tm = TOKEN_TILE
    d_qk = N_HEADS * DH_QK
    d_v = N_HEADS * DH_V
    n_gate = 4 * N_HEADS
    n_rows = 3 * N_HEADS
    cpt = tm // SCAN_CHUNK
    assert s % tm == 0 and tm % SCAN_CHUNK == 0
    tok = lambda width: pl.BlockSpec((1, tm, width), lambda bi, i: (bi, i, 0))
    rows_spec = pl.BlockSpec((1, cpt, n_rows, SCAN_CHUNK), lambda bi, i: (bi, i, 0, 0))
    rows_shape = jax.ShapeDtypeStruct((b, s // SCAN_CHUNK, n_rows, SCAN_CHUNK), F32)
    cols_shape = jax.ShapeDtypeStruct((b, s, n_gate), F32)
    est = d * (2 * d_qk + 2 * d_v) * 2 + 2 * tm * (d * 4 + (2 * d_qk + d_v) * 2 + d_v * 4) + 4 * tm * d * 4
    return pl.pallas_call(
        _mlstm_in_body,
        grid=(b, s // tm),
        in_specs=[tok(d), _resident((1, d)), _resident((d, 2 * d_qk + 2 * d_v)),
                  _resident((n_gate, d)), _resident((n_gate, 1))],
        out_specs=[tok(d_qk), tok(d_qk), tok(d_v), tok(d_v), rows_spec, rows_spec, tok(n_gate), tok(n_gate)],
        out_shape=[jax.ShapeDtypeStruct((b, s, d_qk), BF16), jax.ShapeDtypeStruct((b, s, d_qk), BF16),
                   jax.ShapeDtypeStruct((b, s, d_v), BF16), jax.ShapeDtypeStruct((b, s, d_v), F32),
                   rows_shape, rows_shape, cols_shape, cols_shape],
        compiler_params=pltpu.CompilerParams(
            dimension_semantics=("parallel", "parallel"), vmem_limit_bytes=_vmem_limit(est)),
        name="mlstm_in",
    )(x, g_norm.reshape(1, d), w_in, w_gate.T, b_gate.reshape(n_gate, 1))


def _scan_chunk(forward, q, k, v, rows, cols, c_ref, m_ref, h_ref, row0):
    nh = N_HEADS
    length = q.shape[0]
    ti = lax.broadcasted_iota(jnp.int32, (length, length), 0)
    ji = lax.broadcasted_iota(jnp.int32, (length, length), 1)
    seen = (ji <= ti) if forward else (ji >= ti)

    r = rows[0:nh, :]
    g_tot = rows[nh:2 * nh, :]
    m_loc = rows[2 * nh:, :]
    m_prev = m_ref[...]
    m_new = jnp.maximum(g_tot + m_prev, m_loc)
    s_old = jnp.exp2(g_tot + m_prev - m_new)
    s_loc = jnp.exp2(m_loc - m_new)
    tn =(((0,), (0,)), ((), ()))
    nt = (((1,), (1,)), ((), ()))

    tile = (length, 2 * DH_QK)
    low = lax.broadcasted_iota(jnp.int32, tile, 1) < DH_QK
    ones_blk = jnp.ones((length, DH_V), BF16)
    eye = (ti == ji).astype(BF16)

    def issue(p):
        qp = q[:, 2 * DH_QK * p:2 * DH_QK * (p + 1)]
        kp = k[:, 2 * DH_QK * p:2 * DH_QK * (p + 1)]
        k_eye = jnp.concatenate([kp, eye], axis=0)
        s_exts = [lax.dot_general(jnp.where(m, qp, jnp.zeros_like(qp)), k_eye, nt, preferred_element_type=F32)
                  for m in (low, jnp.logical_not(low))]
        v_augs = [jnp.concatenate([v[:, DH_V * hd:DH_V * (hd + 1)], ones_blk], axis=1)
                  for hd in (2 * p, 2 * p + 1)]
        kf = kp.astype(F32)
        w0 = jnp.broadcast_to(cols[:, 2 * nh + 2 * p:2 * nh + 2 * p + 1], tile)
        w1 = jnp.broadcast_to(cols[:, 2 * nh + 2 * p + 1:2 * nh + 2 * p + 2], tile)
        kw = jnp.concatenate([kf * jnp.where(low, w0, 0.0), kf * jnp.where(low, 0.0, w1)], axis=0)
        c_loc = lax.dot_general(kw.astype(BF16), jnp.concatenate(v_augs, axis=0), tn,
                                preferred_element_type=F32)
        return s_exts, v_augs, c_loc

    def finish(p, s_exts, v_augs, c_loc):
        c_st = c_ref[p]
        c_bf = c_st.astype(BF16)
        for x in range(2):
            hd = 2 * p + x
            mp = m_prev[hd:hd + 1, :]
            neg_m = jnp.minimum(jnp.broadcast_to(cols[:, hd:hd + 1], tile), -mp)
            arg = jnp.concatenate([jnp.where(seen, neg_m + r[hd:hd + 1, :], -jnp.inf), neg_m + mp], axis=1)
            p_ext = (s_exts[x] * jnp.exp2(arg)).astype(BF16)
            nd = _dot(p_ext, jnp.concatenate([v_augs[x], c_bf], axis=0))
            neg_b = jnp.broadcast_to(cols[:, nh + hd:nh + hd + 1], tile)
            floor = jnp.exp2(neg_b + neg_m)
            den = jnp.maximum(jnp.abs(nd[:, DH_V:]), floor)
            h_ref[0, pl.ds(row0, length), DH_V * hd:DH_V * (hd + 1)] = nd[:, 0:DH_V] / den
        per_head = (DH_QK, length)
        so = jnp.concatenate([jnp.broadcast_to(s_old[2 * p:2 * p + 1, :], per_head),
                              jnp.broadcast_to(s_old[2 * p + 1:2 * p + 2, :], per_head)], axis=0)
        sl = jnp.concatenate([jnp.broadcast_to(s_loc[2 * p:2 * p + 1, :], per_head),
                              jnp.broadcast_to(s_loc[2 * p + 1:2 * p + 2, :], per_head)], axis=0)
        c_ref[p] = jnp.concatenate([so, so], axis=1) * c_st + jnp.concatenate([sl, sl], axis=1) * c_loc

    for p in range(nh // 2):
        finish(p, *issue(p))
    m_ref[...] = m_new


def _mlstm_scan_body(qf_ref, kf_ref, vf_ref, rowsf_ref, colsf_ref, qb_ref, kb_ref, vb_ref, rowsb_ref, colsb_ref,
                     hf_ref, hb_ref, cf_scr, cb_scr, mf_scr, mb_scr):
    @pl.when(pl.program_id(1) == 0)
    def _():
        cf_scr[...] = jnp.zeros_like(cf_scr)
        cb_scr[...] = jnp.zeros_like(cb_scr)
        mf_scr[...] = jnp.zeros_like(mf_scr)
        mb_scr[...] = jnp.zeros_like(mb_scr)

    length = SCAN_CHUNK
    n_chunks = qf_ref.shape[1] // length

    def body(c, carry):
        rf = pl.multiple_of(c * length, length)
        _scan_chunk(True, qf_ref[0, pl.ds(rf, length), :], kf_ref[0, pl.ds(rf, length), :],
                    vf_ref[0, pl.ds(rf, length), :], rowsf_ref[0, c], colsf_ref[0, pl.ds(rf, length), :],
                    cf_scr, mf_scr, hf_ref, rf)
        cb = n_chunks - 1 - c
        rb = pl.multiple_of(cb * length, length)
        _scan_chunk(False, qb_ref[0, pl.ds(rb, length), :], kb_ref[0, pl.ds(rb, length), :],
                    vb_ref[0, pl.ds(rb, length), :], rowsb_ref[0, cb], colsb_ref[0, pl.ds(rb, length), :],
                    cb_scr, mb_scr, hb_ref, rb)
        return carry

    lax.fori_loop(0, n_chunks, body, 0)


def _mlstm_scan(q, k, v, rows_fw, rows_bw, cols_fw, cols_bw):
    b, s, d_qk = q.shape
    d_v = v.shape[2]
    tb = SCAN_BLOCK
    nb = s // tb
    cpb = tb // SCAN_CHUNK
    n_rows = rows_fw.shape[2]
    n_cols = cols_fw.shape[2]
    assert s % tb == 0 and tb % SCAN_CHUNK == 0
    assert SCAN_CHUNK == V7X_LANES and 2 * DH_QK == V7X_LANES and DH_V == V7X_LANES
    fw = lambda width: pl.BlockSpec((1, tb, width), lambda bi, i: (bi, i, 0))
    bw = lambda width: pl.BlockSpec((1, tb, width), lambda bi, i: (bi, nb - 1 - i, 0))
    rows_f = pl.BlockSpec((1, cpb, n_rows, SCAN_CHUNK), lambda bi, i: (bi, i, 0, 0))
    rows_b = pl.BlockSpec((1, cpb, n_rows, SCAN_CHUNK), lambda bi, i: (bi, nb - 1 - i, 0, 0))
    state = pltpu.VMEM((N_HEADS // 2, 2 * DH_QK, 2 * DH_V), F32)
    stab = pltpu.VMEM((N_HEADS, SCAN_CHUNK), F32)
    est = 2 * 2 * tb * ((2 * d_qk + d_v) * 2 + d_v * 4 + (n_rows + V7X_LANES) * 4) + 16 * SCAN_CHUNK * 1024 * 4
    return pl.pallas_call(
        _mlstm_scan_body,
        grid=(b, nb),
        in_specs=[fw(d_qk), fw(d_qk), fw(d_v), rows_f, fw(n_cols),
                  bw(d_qk), bw(d_qk), bw(d_v), rows_b, bw(n_cols)],
        out_specs=[fw(d_v), bw(d_v)],
        out_shape=[jax.ShapeDtypeStruct((b, s, d_v), F32), jax.ShapeDtypeStruct((b, s, d_v), F32)],
        scratch_shapes=[state, state, stab, stab],
        compiler_params=pltpu.CompilerParams(
            dimension_semantics=("arbitrary", "arbitrary"), vmem_limit_bytes=_vmem_limit(est)),
        name="mlstm_scan",
    )(q, k, v, rows_fw, cols_fw, q, k, v, rows_bw, cols_bw)


def _trunk(x, p):
    b, s, d = x.shape
    flat = lambda a: a.reshape(b * s, a.shape[-1])
    depth = p["g_ffn1"].shape[0]
    for layer in range(depth):
        j = layer // 2
        x = _ffn(flat(x), p["g_ffn1"][layer], p["w_ffn1_gate"][layer], p["w_ffn1_up"][layer],
                 p["w_ffn1_down"][layer]).reshape(b, s, d)
        mlstm = None
        if layer % 2 == 0:
            x = _conv_mixer(x, p["g_mix"][layer], p["w_conv_in"][j], p["w_conv_dw"][j], p["w_conv_out"][j])
        else:
            q, k, v, o, rows_fw, rows_bw, cols_fw, cols_bw = _mlstm_in(
                x, p["g_mix"][layer], p["w_mlstm_in"][j], p["w_mlstm_gate"][j], p["b_mlstm_gate"][j])
            h_fw, h_bw = _mlstm_scan(q, k, v, rows_fw, rows_bw, cols_fw, cols_bw)
            mlstm = (flat(h_fw), flat(h_bw), flat(o), p["g_mlstm_head"][j], p["w_mlstm_out"][j])
        g_final = p["g_final"] if layer == depth - 1 else None
        x = _ffn(flat(x), p["g_ffn2"][layer], p["w_ffn2_gate"][layer], p["w_ffn2_up"][layer],
                 p["w_ffn2_down"][layer], mlstm, g_final).reshape(b, s, d)
    return x


def kernel(x_prompt, x_sample, g_ffn1, w_ffn1_gate, w_ffn1_up, w_ffn1_down, g_mix, w_conv_in, w_conv_dw, w_conv_out, w_mlstm_in, w_mlstm_gate, b_mlstm_gate, g_mlstm_head, w_mlstm_out, g_ffn2, w_ffn2_gate, w_ffn2_up, w_ffn2_down, g_final):
    bf = lambda w: w.astype(BF16)
    p = dict(
        g_ffn1=g_ffn1, w_ffn1_gate=bf(w_ffn1_gate), w_ffn1_up=bf(w_ffn1_up), w_ffn1_down=bf(w_ffn1_down),
        g_mix=g_mix, w_conv_in=bf(w_conv_in), w_conv_dw=w_conv_dw, w_conv_out=bf(w_conv_out),
        w_mlstm_in=bf(w_mlstm_in), w_mlstm_gate=bf(w_mlstm_gate), b_mlstm_gate=b_mlstm_gate,
        g_mlstm_head=g_mlstm_head, w_mlstm_out=bf(w_mlstm_out),
        g_ffn2=g_ffn2, w_ffn2_gate=bf(w_ffn2_gate), w_ffn2_up=bf(w_ffn2_up), w_ffn2_down=bf(w_ffn2_down),
        g_final=g_final)
    return _trunk(x_prompt, p), _trunk(x_sample, p)
```

```python
import functools

import jax
import jax.numpy as jnp
from jax import lax
from jax.experimental import pallas as pl
from jax.experimental.pallas import tpu as pltpu

F32 = jnp.float32
BF16 = jnp.bfloat16

N_HEADS = 8
DH_QK = 64
DH_V = 128
EPS = 1e-6
LOG2_E = 1.4426950408889634

V7X_VMEM_BYTES = 64 * 1024 * 1024
V7X_LANES = 128
BF16_SUBLANES = 16

TOKEN_TILE = 512
FFN_TILE = 1024
FFN_COL_CHUNK = 256
SCAN_CHUNK = 256
SCAN_BLOCK = 512
CONV_HALO = BF16_SUBLANES


def _vmem_limit(estimate_bytes):
    return int(min(max(2 * estimate_bytes, 32 * 1024 * 1024), V7X_VMEM_BYTES - 8 * 1024 * 1024))


def _resident(shape):
    zeros = (0,) * len(shape)
    return pl.BlockSpec(shape, lambda *_: zeros, pipeline_mode=pl.Buffered(1))


def _layer_param(stack, layer):
    if stack.ndim == 2:
        stack = stack.reshape(stack.shape[0], 1, stack.shape[1])
    zeros = (0,) * (stack.ndim - 1)
    spec = pl.BlockSpec((None,) + stack.shape[1:], lambda *_: (layer,) + zeros, pipeline_mode=pl.Buffered(1))
    return spec, stack


def _rmsnorm(x, g):
    ms = jnp.mean(x * x, axis=-1, keepdims=True)
    return x * lax.rsqrt(ms + EPS) * g


def _dot(a, b):
    return jnp.dot(a, b, preferred_element_type=F32)


def _mlstm_output(h, o, g_head, w_out):
    parts = []
    for hd in range(N_HEADS):
        hh = h[:, DH_V * hd:DH_V * (hd + 1)]
        parts.append(hh * lax.rsqrt(jnp.mean(hh * hh, axis=-1, keepdims=True) + EPS))
    hn = jnp.concatenate(parts, axis=1) * g_head
    return _dot((hn * jax.nn.sigmoid(o)).astype(BF16), w_out)


def _ffn_body(*refs, mlstm_pre, final_norm):
    refs = list(refs)
    x_ref, gn_ref, wg_ref, wu_ref, wd_ref = refs[:5]
    o_ref, h_scr = refs[-2:]
    extra = refs[5:-2]
    if mlstm_pre:
        hf_ref, hb_ref, og_ref, gh_ref, wo_ref = extra[:5]
    if final_norm:
        gf_ref = extra[-1]
    d_ff = wg_ref.shape[1]
    x = x_ref[...]
    if mlstm_pre:
        x = x + _mlstm_output(hf_ref[...] + hb_ref[...], og_ref[...], gh_ref[...], wo_ref[...])
    xn = _rmsnorm(x, gn_ref[...]).astype(BF16)
    for c in range(d_ff // FFN_COL_CHUNK):
        sl = slice(c * FFN_COL_CHUNK, (c + 1) * FFN_COL_CHUNK)
        gate = _dot(xn, wg_ref[:, sl])
        up = _dot(xn, wu_ref[:, sl])
        h_scr[:, sl] = (gate * jax.nn.sigmoid(gate) * up).astype(BF16)
    out = x + 0.5 * _dot(h_scr[...], wd_ref[...])
    if final_norm:
        out = _rmsnorm(out, gf_ref[...])
    o_ref[...] = out


def _ffn(x2d, layer, g_norm, w_gate, w_up, w_down, mlstm=None, g_final=None):
    t, d = x2d.shape
    d_ff = w_gate.shape[2]
    tm = TOKEN_TILE if mlstm is not None else FFN_TILE
    assert t % tm == 0 and d_ff % FFN_COL_CHUNK == 0
    final_norm = g_final is not None
    row = pl.BlockSpec((tm, d), lambda i: (i, 0))
    params = [_layer_param(a, layer) for a in (g_norm, w_gate, w_up, w_down)]
    in_specs = [row] + [spec for spec, _ in params]
    args = [x2d] + [a for _, a in params]
    est = 3 * d * d_ff * 2 + 4 * tm * d * 4 + tm * d_ff * 2 + 4 * tm * d * 4
    if mlstm is not None:
        h_fw, h_bw, o, g_head, w_out, j = mlstm
        d_v = h_fw.shape[1]
        row_v = pl.BlockSpec((tm, d_v), lambda i: (i, 0))
        params = [_layer_param(a, j) for a in (g_head, w_out)]
        in_specs += [row_v, row_v, row_v] + [spec for spec, _ in params]
        args += [h_fw, h_bw, o] + [a for _, a in params]
        est += d_v * d * 2 + 8 * tm * d_v * 4
    if final_norm:
        in_specs.append(_resident((1, d)))
        args.append(g_final.reshape(1, d))
    return pl.pallas_call(
        functools.partial(_ffn_body, mlstm_pre=mlstm is not None, final_norm=final_norm),
        grid=(t // tm,),
        in_specs=in_specs,
        out_specs=row,
        out_shape=jax.ShapeDtypeStruct((t, d), F32),
        scratch_shapes=[pltpu.VMEM((tm, d_ff), BF16)],
        compiler_params=pltpu.CompilerParams(
            dimension_semantics=("parallel",), vmem_limit_bytes=_vmem_limit(est)),
        name="ffn",
    )(*args)


def _conv_body(xm_ref, xp_ref, xn_ref, g_ref, win_ref, wdw_ref, wout_ref, o_ref, h_scr):
    i = pl.program_id(1)
    n = pl.num_programs(1)
    tm = xm_ref.shape[1]
    d = xm_ref.shape[2]
    halo = CONV_HALO
    g = g_ref[...]
    xm = xm_ref[0]
    h_scr[0:halo] = _rmsnorm(xp_ref[0], g).astype(BF16)
    h_scr[halo:halo + tm] = _rmsnorm(xm, g).astype(BF16)
    h_scr[halo + tm:] = _rmsnorm(xn_ref[0], g).astype(BF16)
    h = h_scr[...]
    z = _dot(h, win_ref[:, d:2 * d]) * _dot(h, win_ref[:, 2 * d:])
    row = lax.broadcasted_iota(jnp.int32, (tm + 2 * halo, 1), 0)
    inside = jnp.logical_and(jnp.logical_or(row >= halo, i > 0),
                             jnp.logical_or(row < halo + tm, i < n - 1))
    z = jnp.where(inside, z, 0.0)
    rows = tm + 2 * halo
    z_prev = pltpu.roll(z, 1, axis=0)[halo:halo + tm]
    z_next = pltpu.roll(z, rows - 1, axis=0)[halo:halo + tm]
    y = wdw_ref[0:1] * z_prev + wdw_ref[1:2] * z[halo:halo + tm] + wdw_ref[2:3] * z_next
    b_gate = _dot(h_scr[halo:halo + tm], win_ref[:, 0:d])
    o_ref[0] = xm + _dot((b_gate * y).astype(BF16), wout_ref[...])


def _conv_mixer(x, layer, j, g_norm, w_in, w_dw, w_out):
    b, s, d = x.shape
    tm = TOKEN_TILE
    halo = CONV_HALO
    assert s % tm == 0 and tm % halo == 0
    per = tm // halo
    last = s // halo - 1
    main = pl.BlockSpec((1, tm, d), lambda bi, i: (bi, i, 0))
    prev = pl.BlockSpec((1, halo, d), lambda bi, i: (bi, jnp.maximum(i * per - 1, 0), 0))
    nxt = pl.BlockSpec((1, halo, d), lambda bi, i: (bi, jnp.minimum((i + 1) * per, last), 0))
    est = 4 * d * d * 2 + 4 * tm * d * 4 + 8 * (tm + 2 * halo) * d * 4
    params = [_layer_param(g_norm, layer)] + [_layer_param(a, j) for a in (w_in, w_dw, w_out)]
    return pl.pallas_call(
        _conv_body,
        grid=(b, s // tm),
        in_specs=[main, prev, nxt] + [spec for spec, _ in params],
        out_specs=main,
        out_shape=jax.ShapeDtypeStruct((b, s, d), F32),
        scratch_shapes=[pltpu.VMEM((tm + 2 * halo, d), BF16)],
        compiler_params=pltpu.CompilerParams(
            dimension_semantics=("parallel", "parallel"), vmem_limit_bytes=_vmem_limit(est)),
        name="conv_mixer",
    )(x, x, x, *[a for _, a in params])


def _log_sigmoid(x):
    return jnp.minimum(x, 0.0) - jnp.log1p(jnp.exp(-jnp.abs(x)))


def _lane_scan(x, op, identity, forward):
    length = x.shape[1]
    lane = lax.broadcasted_iota(jnp.int32, x.shape, 1)
    shift = 1
    while shift < length:
        if forward:
            moved = pltpu.roll(x, shift, axis=1)
            valid = lane >= shift
        else:
            moved = pltpu.roll(x, length - shift, axis=1)
            valid = lane < length - shift
        x = op(x, jnp.where(valid, moved, identity))
        shift *= 2
    return x


def _bf16_pieces(x, n):
    pieces = []
    for _ in range(n):
        piece = x.astype(BF16).astype(F32)
        pieces.append(piece)
        x = x - piece
    return pieces


def _chunk_gate_features(gates, forward):
    nh = N_HEADS
    length = gates.shape[1]
    logf = _log_sigmoid(gates[nh:, :]) * LOG2_E
    b = _lane_scan(logf, jnp.add, 0.0, forward)
    r = gates[0:nh, :] * LOG2_E - b
    cm = _lane_scan(r, jnp.maximum, -jnp.inf, forward)
    g_tot = jnp.broadcast_to(jnp.sum(logf, axis=1, keepdims=True), (nh, length))
    a = g_tot + r
    m_loc = jnp.broadcast_to(jnp.max(a, axis=1, keepdims=True), (nh, length))
    w_loc = jnp.exp2(a - m_loc)
    pieces = _bf16_pieces(-cm, 3) + _bf16_pieces(-b, 3) + _bf16_pieces(w_loc, 2)
    stacked = jnp.concatenate(pieces, axis=0).astype(BF16)
    return jnp.concatenate([r, g_tot, m_loc], axis=0), stacked


def _pieces_to_cols(stacked):
    nh = N_HEADS
    n_rows = stacked.shape[0]
    prow = lax.broadcasted_iota(jnp.int32, (n_rows, 4 * nh), 0)
    pcol = lax.broadcasted_iota(jnp.int32, (n_rows, 4 * nh), 1)
    quantity = (prow >= 3 * nh).astype(jnp.int32) + (prow >= 6 * nh).astype(jnp.int32)
    gather = (pcol == (prow % nh) + nh * quantity).astype(BF16)
    return lax.dot_general(stacked, gather, (((0,), (0,)), ((), ())), preferred_element_type=F32)


def _mlstm_in_body(x_ref, g_ref, win_ref, wgate_t_ref, bgate_t_ref,
                   q_ref, k_ref, v_ref, o_ref, rows_fw_ref, rows_bw_ref, cols_fw_ref, cols_bw_ref):
    d_qk = q_ref.shape[2]
    d_v = v_ref.shape[2]
    h = _rmsnorm(x_ref[0], g_ref[...]).astype(BF16)
    gates_t = lax.dot_general(wgate_t_ref[...], h, (((1,), (1,)), ((), ())),
                              preferred_element_type=F32) + bgate_t_ref[...]
    half = 2 * N_HEADS
    stacked = []
    for c in range(rows_fw_ref.shape[1]):
        sl = slice(c * SCAN_CHUNK, (c + 1) * SCAN_CHUNK)
        rows_fw_ref[0, c], st_fw = _chunk_gate_features(gates_t[0:half, sl], True)
        rows_bw_ref[0, c], st_bw = _chunk_gate_features(gates_t[half:, sl], False)
        stacked.append((sl, st_fw, st_bw))
    q_ref[0] = _dot(h, win_ref[:, 0:d_qk]).astype(BF16)
    k_ref[0] = (_dot(h, win_ref[:, d_qk:2 * d_qk]) * (DH_QK ** -0.5)).astype(BF16)
    v_ref[0] = _dot(h, win_ref[:, 2 * d_qk:2 * d_qk + d_v]).astype(BF16)
    o_ref[0] = _dot(h, win_ref[:, 2 * d_qk + d_v:])
    for sl, st_fw, st_bw in stacked:
        cols_fw_ref[0, sl, :] = _pieces_to_cols(st_fw)
        cols_bw_ref[0, sl, :] = _pieces_to_cols(st_bw)


def _mlstm_in(x, layer, j, g_norm, w_in, w_gate_t, b_gate_t):
    b, s, d = x.shape
    tm = TOKEN_TILE
    d_qk = N_HEADS * DH_QK
    d_v = N_HEADS * DH_V
    n_gate = 4 * N_HEADS
    n_rows = 3 * N_HEADS
    cpt = tm // SCAN_CHUNK
    assert s % tm == 0 and tm % SCAN_CHUNK == 0
    tok = lambda width: pl.BlockSpec((1, tm, width), lambda bi, i: (bi, i, 0))
    rows_spec = pl.BlockSpec((1, cpt, n_rows, SCAN_CHUNK), lambda bi, i: (bi, i, 0, 0))
    rows_shape = jax.ShapeDtypeStruct((b, s // SCAN_CHUNK, n_rows, SCAN_CHUNK), F32)
    cols_shape = jax.ShapeDtypeStruct((b, s, n_gate), F32)
    est = d * (2 * d_qk + 2 * d_v) * 2 + 2 * tm * (d * 4 + (2 * d_qk + d_v) * 2 + d_v * 4) + 4 * tm * d * 4
    params = [_layer_param(g_norm, layer)] + [_layer_param(a, j) for a in (w_in, w_gate_t, b_gate_t)]
    return pl.pallas_call(
        _mlstm_in_body,
        grid=(b, s // tm),
        in_specs=[tok(d)] + [spec for spec, _ in params],
        out_specs=[tok(d_qk), tok(d_qk), tok(d_v), tok(d_v), rows_spec, rows_spec, tok(n_gate), tok(n_gate)],
        out_shape=[jax.ShapeDtypeStruct((b, s, d_qk), BF16), jax.ShapeDtypeStruct((b, s, d_qk), BF16),
                   jax.ShapeDtypeStruct((b, s, d_v), BF16), jax.ShapeDtypeStruct((b, s, d_v), F32),
                   rows_shape, rows_shape, cols_shape, cols_shape],
        compiler_params=pltpu.CompilerParams(
            dimension_semantics=("parallel", "parallel"), vmem_limit_bytes=_vmem_limit(est)),
        name="mlstm_in",
    )(x, *[a for _, a in params])


def _scan_chunk(forward, q, k, v, rows, cols, c_ref, m_ref, h_ref, row0):
    nh = N_HEADS
    length = q.shape[0]
    ti = lax.broadcasted_iota(jnp.int32, (length, length), 0)
    ji = lax.broadcasted_iota(jnp.int32, (length, length), 1)
    seen = (ji <= ti) if forward else (ji >= ti)

    r = rows[0:nh, :]
    g_tot = rows[nh:2 * nh, :]
    m_loc = rows[2 * nh:, :]
    m_prev = m_ref[...]
    m_new = jnp.maximum(g_tot + m_prev, m_loc)
    s_old = jnp.exp2(g_tot + m_prev - m_new)
    s_loc = jnp.exp2(m_loc - m_new)
    tn =(((0,), (0,)), ((), ()))
    nt = (((1,), (1,)), ((), ()))

    qk_tile = (length, 2 * DH_QK)
    low = lax.broadcasted_iota(jnp.int32, qk_tile, 1) < DH_QK
    ones_blk = jnp.ones((length, DH_V), BF16)
    eye = (lax.broadcasted_iota(jnp.int32, (2 * DH_QK, 2 * DH_QK), 0)
           == lax.broadcasted_iota(jnp.int32, (2 * DH_QK, 2 * DH_QK), 1)).astype(BF16)

    def v_aug(hd):
        return jnp.concatenate([v[:, DH_V * hd:DH_V * (hd + 1)], ones_blk], axis=1)

    def outputs(hd):
        p = hd // 2
        qp = q[:, 2 * DH_QK * p:2 * DH_QK * (p + 1)]
        kp = k[:, 2 * DH_QK * p:2 * DH_QK * (p + 1)]
        qm = jnp.where(low if hd % 2 == 0 else jnp.logical_not(low), qp, jnp.zeros_like(qp))
        s_ext = lax.dot_general(qm, jnp.concatenate([kp, eye], axis=0), nt, preferred_element_type=F32)
        mp = m_prev[hd:hd + 1, :]
        neg_m = jnp.minimum(jnp.broadcast_to(cols[:, hd:hd + 1], (length, length)), -mp)
        neg_m_q = neg_m[:, 0:2 * DH_QK]
        arg = jnp.concatenate([jnp.where(seen, neg_m + r[hd:hd + 1, :], -jnp.inf),
                               neg_m_q + mp[:, 0:2 * DH_QK]], axis=1)
        p_ext = (s_ext * jnp.exp2(arg)).astype(BF16)
        nd = _dot(p_ext, jnp.concatenate([v_aug(hd), c_ref[p].astype(BF16)], axis=0))
        neg_b = jnp.broadcast_to(cols[:, nh + hd:nh + hd + 1], (length, DH_V))
        floor = jnp.exp2(neg_b + neg_m[:, 0:DH_V])
        den = jnp.maximum(jnp.abs(nd[:, DH_V:]), floor)
        h_ref[0, pl.ds(row0, length), DH_V * hd:DH_V * (hd + 1)] = nd[:, 0:DH_V] / den

    def update_state(p):
        kf = k[:, 2 * DH_QK * p:2 * DH_QK * (p + 1)].astype(F32)
        w0 = jnp.broadcast_to(cols[:, 2 * nh + 2 * p:2 * nh + 2 * p + 1], qk_tile)
        w1 = jnp.broadcast_to(cols[:, 2 * nh + 2 * p + 1:2 * nh + 2 * p + 2], qk_tile)
        kw = jnp.concatenate([kf * jnp.where(low, w0, 0.0), kf * jnp.where(low, 0.0, w1)], axis=0)
        c_loc = lax.dot_general(kw.astype(BF16), jnp.concatenate([v_aug(2 * p), v_aug(2 * p + 1)], axis=0), tn,
                                preferred_element_type=F32)

        def per_row(scale):
            halves = [jnp.broadcast_to(scale[hd:hd + 1, 0:DH_V], (DH_QK, DH_V)) for hd in (2 * p, 2 * p + 1)]
            half = jnp.concatenate(halves, axis=0)
            return jnp.concatenate([half, half], axis=1)

        c_ref[p] = per_row(s_old) * c_ref[p] + per_row(s_loc) * c_loc

    for hd in range(nh):
        outputs(hd)
        if hd % 2 == 1:
            update_state(hd // 2)
    m_ref[...] = m_new


def _mlstm_scan_body(qf_ref, kf_ref, vf_ref, rowsf_ref, colsf_ref, qb_ref, kb_ref, vb_ref, rowsb_ref, colsb_ref,
                     hf_ref, hb_ref, cf_scr, cb_scr, mf_scr, mb_scr):
    @pl.when(pl.program_id(1) == 0)
    def _():
        cf_scr[...] = jnp.zeros_like(cf_scr)
        cb_scr[...] = jnp.zeros_like(cb_scr)
        mf_scr[...] = jnp.zeros_like(mf_scr)
        mb_scr[...] = jnp.zeros_like(mb_scr)

    length = SCAN_CHUNK
    n_chunks = qf_ref.shape[1] // length

    def body(c, carry):
        rf = pl.multiple_of(c * length, length)
        _scan_chunk(True, qf_ref[0, pl.ds(rf, length), :], kf_ref[0, pl.ds(rf, length), :],
                    vf_ref[0, pl.ds(rf, length), :], rowsf_ref[0, c], colsf_ref[0, pl.ds(rf, length), :],
                    cf_scr, mf_scr, hf_ref, rf)
        cb = n_chunks - 1 - c
        rb = pl.multiple_of(cb * length, length)
        _scan_chunk(False, qb_ref[0, pl.ds(rb, length), :], kb_ref[0, pl.ds(rb, length), :],
                    vb_ref[0, pl.ds(rb, length), :], rowsb_ref[0, cb], colsb_ref[0, pl.ds(rb, length), :],
                    cb_scr, mb_scr, hb_ref, rb)
        return carry

    lax.fori_loop(0, n_chunks, body, 0)


def _mlstm_scan(q, k, v, rows_fw, rows_bw, cols_fw, cols_bw):
    b, s, d_qk = q.shape
    d_v = v.shape[2]
    tb = SCAN_BLOCK
    nb = s // tb
    cpb = tb // SCAN_CHUNK
    n_rows = rows_fw.shape[2]
    n_cols = cols_fw.shape[2]
    assert s % tb == 0 and tb % SCAN_CHUNK == 0
    assert SCAN_CHUNK % V7X_LANES == 0 and 2 * DH_QK == V7X_LANES and DH_V == V7X_LANES
    fw = lambda width: pl.BlockSpec((1, tb, width), lambda bi, i: (bi, i, 0))
    bw = lambda width: pl.BlockSpec((1, tb, width), lambda bi, i: (bi, nb - 1 - i, 0))
    rows_f = pl.BlockSpec((1, cpb, n_rows, SCAN_CHUNK), lambda bi, i: (bi, i, 0, 0))
    rows_b = pl.BlockSpec((1, cpb, n_rows, SCAN_CHUNK), lambda bi, i: (bi, nb - 1 - i, 0, 0))
    state = pltpu.VMEM((N_HEADS // 2, 2 * DH_QK, 2 * DH_V), F32)
    stab = pltpu.VMEM((N_HEADS, SCAN_CHUNK), F32)
    est = 2 * 2 * tb * ((2 * d_qk + d_v) * 2 + d_v * 4 + (n_rows + V7X_LANES) * 4) + 16 * SCAN_CHUNK * 1024 * 4
    return pl.pallas_call(
        _mlstm_scan_body,
        grid=(b, nb),
        in_specs=[fw(d_qk), fw(d_qk), fw(d_v), rows_f, fw(n_cols),
                  bw(d_qk), bw(d_qk), bw(d_v), rows_b, bw(n_cols)],
        out_specs=[fw(d_v), bw(d_v)],
        out_shape=[jax.ShapeDtypeStruct((b, s, d_v), F32), jax.ShapeDtypeStruct((b, s, d_v), F32)],
        scratch_shapes=[state, state, stab, stab],
        compiler_params=pltpu.CompilerParams(
            dimension_semantics=("arbitrary", "arbitrary"), vmem_limit_bytes=_vmem_limit(est)),
        name="mlstm_scan",
    )(q, k, v, rows_fw, cols_fw, q, k, v, rows_bw, cols_bw)


def _trunk(x, p):
    b, s, d = x.shape
    flat = lambda a: a.reshape(b * s, a.shape[-1])
    depth = p["g_ffn1"].shape[0]
    for layer in range(depth):
        j = layer // 2
        x = _ffn(flat(x), layer, p["g_ffn1"], p["w_ffn1_gate"], p["w_ffn1_up"], p["w_ffn1_down"]).reshape(b, s, d)
        mlstm = None
        if layer % 2 == 0:
            x = _conv_mixer(x, layer, j, p["g_mix"], p["w_conv_in"], p["w_conv_dw"], p["w_conv_out"])
        else:
            q, k, v, o, rows_fw, rows_bw, cols_fw, cols_bw = _mlstm_in(
                x, layer, j, p["g_mix"], p["w_mlstm_in"], p["w_mlstm_gate_t"], p["b_mlstm_gate_t"])
            h_fw, h_bw = _mlstm_scan(q, k, v, rows_fw, rows_bw, cols_fw, cols_bw)
            mlstm = (flat(h_fw), flat(h_bw), flat(o), p["g_mlstm_head"], p["w_mlstm_out"], j)
        g_final = p["g_final"] if layer == depth - 1 else None
        x = _ffn(flat(x), layer, p["g_ffn2"], p["w_ffn2_gate"], p["w_ffn2_up"], p["w_ffn2_down"],
                 mlstm, g_final).reshape(b, s, d)
    return x


def kernel(x_prompt, x_sample, g_ffn1, w_ffn1_gate, w_ffn1_up, w_ffn1_down, g_mix, w_conv_in, w_conv_dw, w_conv_out, w_mlstm_in, w_mlstm_gate, b_mlstm_gate, g_mlstm_head, w_mlstm_out, g_ffn2, w_ffn2_gate, w_ffn2_up, w_ffn2_down, g_final):
    bf = lambda w: w.astype(BF16)
    p = dict(
        g_ffn1=g_ffn1, w_ffn1_gate=bf(w_ffn1_gate), w_ffn1_up=bf(w_ffn1_up), w_ffn1_down=bf(w_ffn1_down),
        g_mix=g_mix, w_conv_in=bf(w_conv_in), w_conv_dw=w_conv_dw, w_conv_out=bf(w_conv_out),
        w_mlstm_in=bf(w_mlstm_in), w_mlstm_gate_t=bf(jnp.swapaxes(w_mlstm_gate, 1, 2)),
        b_mlstm_gate_t=b_mlstm_gate[:, :, None],
        g_mlstm_head=g_mlstm_head, w_mlstm_out=bf(w_mlstm_out),
        g_ffn2=g_ffn2, w_ffn2_gate=bf(w_ffn2_gate), w_ffn2_up=bf(w_ffn2_up), w_ffn2_down=bf(w_ffn2_down),
        g_final=g_final)
    return _trunk(x_prompt, p), _trunk(x_sample, p)
```

```python
import functools

import jax
import jax.numpy as jnp
from jax import lax
from jax.experimental import pallas as pl
from jax.experimental.pallas import tpu as pltpu

F32 = jnp.float32
BF16 = jnp.bfloat16

N_HEADS = 8
DH_QK = 64
DH_V = 128
EPS = 1e-6
LOG2_E = 1.4426950408889634

V7X_VMEM_BYTES = 64 * 1024 * 1024
V7X_LANES = 128
BF16_SUBLANES = 16

TOKEN_TILE = 512
WIDE_TILE = 1024
FFN_COL_CHUNK = 256
SCAN_CHUNK = 256
SCAN_BLOCK = 512
CONV_HALO = BF16_SUBLANES


def _vmem_limit(estimate_bytes):
    return int(min(max(2 * estimate_bytes, 32 * 1024 * 1024), V7X_VMEM_BYTES - 8 * 1024 * 1024))


def _resident(shape):
    zeros = (0,) * len(shape)
    return pl.BlockSpec(shape, lambda *_: zeros, pipeline_mode=pl.Buffered(1))


def _layer_param(stack, layer):
    if stack.ndim == 2:
        stack = stack.reshape(stack.shape[0], 1, stack.shape[1])
    zeros = (0,) * (stack.ndim - 1)
    spec = pl.BlockSpec((None,) + stack.shape[1:], lambda *_: (layer,) + zeros, pipeline_mode=pl.Buffered(1))
    return spec, stack


def _rmsnorm(x, g):
    ms = jnp.mean(x * x, axis=-1, keepdims=True)
    return x * lax.rsqrt(ms + EPS) * g


def _dot(a, b):
    return jnp.dot(a, b, preferred_element_type=F32)


def _mlstm_output(h, o, g_head, w_out):
    parts = []
    for hd in range(N_HEADS):
        hh = h[:, DH_V * hd:DH_V * (hd + 1)]
        parts.append(hh * lax.rsqrt(jnp.mean(hh * hh, axis=-1, keepdims=True) + EPS))
    hn = jnp.concatenate(parts, axis=1) * g_head
    return _dot((hn * jax.nn.sigmoid(o)).astype(BF16), w_out)


def _ffn_body(*refs, mlstm_pre, final_norm):
    refs = list(refs)
    x_ref, gn_ref, wg_ref, wu_ref, wd_ref = refs[:5]
    o_ref, h_scr = refs[-2:]
    extra = refs[5:-2]
    if mlstm_pre:
        hf_ref, hb_ref, og_ref, gh_ref, wo_ref = extra[:5]
    if final_norm:
        gf_ref = extra[-1]
    d_ff = wg_ref.shape[1]
    x = x_ref[...]
    if mlstm_pre:
        x = x + _mlstm_output(hf_ref[...] + hb_ref[...], og_ref[...], gh_ref[...], wo_ref[...])
    xn = _rmsnorm(x, gn_ref[...]).astype(BF16)
    for c in range(d_ff // FFN_COL_CHUNK):
        sl = slice(c * FFN_COL_CHUNK, (c + 1) * FFN_COL_CHUNK)
        gate = _dot(xn, wg_ref[:, sl])
        up = _dot(xn, wu_ref[:, sl])
        h_scr[:, sl] = (gate * jax.nn.sigmoid(gate) * up).astype(BF16)
    out = x + 0.5 * _dot(h_scr[...], wd_ref[...])
    if final_norm:
        out = _rmsnorm(out, gf_ref[...])
    o_ref[...] = out


def _ffn(x2d, layer, g_norm, w_gate, w_up, w_down, mlstm=None, g_final=None):
    t, d = x2d.shape
    d_ff = w_gate.shape[2]
    tm = TOKEN_TILE if mlstm is not None else WIDE_TILE
    assert t % tm == 0 and d_ff % FFN_COL_CHUNK == 0
    final_norm = g_final is not None
    row = pl.BlockSpec((tm, d), lambda i: (i, 0))
    params = [_layer_param(a, layer) for a in (g_norm, w_gate, w_up, w_down)]
    in_specs = [row] + [spec for spec, _ in params]
    args = [x2d] + [a for _, a in params]
    est = 3 * d * d_ff * 2 + 4 * tm * d * 4 + tm * d_ff * 2 + 4 * tm * d * 4
    if mlstm is not None:
        h_fw, h_bw, o, g_head, w_out, j = mlstm
        d_v = h_fw.shape[1]
        row_v = pl.BlockSpec((tm, d_v), lambda i: (i, 0))
        params = [_layer_param(a, j) for a in (g_head, w_out)]
        in_specs += [row_v, row_v, row_v] + [spec for spec, _ in params]
        args += [h_fw, h_bw, o] + [a for _, a in params]
        est += d_v * d * 2 + 8 * tm * d_v * 4
    if final_norm:
        in_specs.append(_resident((1, d)))
        args.append(g_final.reshape(1, d))
    return pl.pallas_call(
        functools.partial(_ffn_body, mlstm_pre=mlstm is not None, final_norm=final_norm),
        grid=(t // tm,),
        in_specs=in_specs,
        out_specs=row,
        out_shape=jax.ShapeDtypeStruct((t, d), F32),
        scratch_shapes=[pltpu.VMEM((tm, d_ff), BF16)],
        compiler_params=pltpu.CompilerParams(
            dimension_semantics=("parallel",), vmem_limit_bytes=_vmem_limit(est)),
        name="ffn",
    )(*args)


def _conv_body(xm_ref, xp_ref, xn_ref, g_ref, win_ref, wdw_ref, wout_ref, o_ref, h_scr):
    i = pl.program_id(1)
    n = pl.num_programs(1)
    tm = xm_ref.shape[1]
    d = xm_ref.shape[2]
    halo = CONV_HALO
    g = g_ref[...]
    xm = xm_ref[0]
    h_scr[0:halo] = _rmsnorm(xp_ref[0], g).astype(BF16)
    h_scr[halo:halo + tm] = _rmsnorm(xm, g).astype(BF16)
    h_scr[halo + tm:] = _rmsnorm(xn_ref[0], g).astype(BF16)
    h = h_scr[...]
    z = _dot(h, win_ref[:, d:2 * d]) * _dot(h, win_ref[:, 2 * d:])
    row = lax.broadcasted_iota(jnp.int32, (tm + 2 * halo, 1), 0)
    inside = jnp.logical_and(jnp.logical_or(row >= halo, i > 0),
                             jnp.logical_or(row < halo + tm, i < n - 1))
    z = jnp.where(inside, z, 0.0)
    rows = tm + 2 * halo
    z_prev = pltpu.roll(z, 1, axis=0)[halo:halo + tm]
    z_next = pltpu.roll(z, rows - 1, axis=0)[halo:halo + tm]
    y = wdw_ref[0:1] * z_prev + wdw_ref[1:2] * z[halo:halo + tm] + wdw_ref[2:3] * z_next
    b_gate = _dot(h_scr[halo:halo + tm], win_ref[:, 0:d])
    o_ref[0] = xm + _dot((b_gate * y).astype(BF16), wout_ref[...])


def _conv_mixer(x, layer, j, g_norm, w_in, w_dw, w_out):
    b, s, d = x.shape
    tm = TOKEN_TILE
    halo = CONV_HALO
    assert s % tm == 0 and tm % halo == 0
    per = tm // halo
    last = s // halo - 1
    main = pl.BlockSpec((1, tm, d), lambda bi, i: (bi, i, 0))
    prev = pl.BlockSpec((1, halo, d), lambda bi, i: (bi, jnp.maximum(i * per - 1, 0), 0))
    nxt = pl.BlockSpec((1, halo, d), lambda bi, i: (bi, jnp.minimum((i + 1) * per, last), 0))
    est = 4 * d * d * 2 + 4 * tm * d * 4 + 8 * (tm + 2 * halo) * d * 4
    params = [_layer_param(g_norm, layer)] + [_layer_param(a, j) for a in (w_in, w_dw, w_out)]
    return pl.pallas_call(
        _conv_body,
        grid=(b, s // tm),
        in_specs=[main, prev, nxt] + [spec for spec, _ in params],
        out_specs=main,
        out_shape=jax.ShapeDtypeStruct((b, s, d), F32),
        scratch_shapes=[pltpu.VMEM((tm + 2 * halo, d), BF16)],
        compiler_params=pltpu.CompilerParams(
            dimension_semantics=("parallel", "parallel"), vmem_limit_bytes=_vmem_limit(est)),
        name="conv_mixer",
    )(x, x, x, *[a for _, a in params])


def _log_sigmoid(x):
    return jnp.minimum(x, 0.0) - jnp.log1p(jnp.exp(-jnp.abs(x)))


def _lane_scan(x, op, identity, forward):
    length = x.shape[1]
    lane = lax.broadcasted_iota(jnp.int32, x.shape, 1)
    shift = 1
    while shift < length:
        if forward:
            moved = pltpu.roll(x, shift, axis=1)
            valid = lane >= shift
        else:
            moved = pltpu.roll(x, length - shift, axis=1)
            valid = lane < length - shift
        x = op(x, jnp.where(valid, moved, identity))
        shift *= 2
    return x


def _bf16_pieces(x, n):
    pieces = []
    for _ in range(n):
        piece = x.astype(BF16).astype(F32)
        pieces.append(piece)
        x = x - piece
    return pieces


def _chunk_gate_features(gates, forward):
    nh = N_HEADS
    length = gates.shape[1]
    logf = _log_sigmoid(gates[nh:, :]) * LOG2_E
    b = _lane_scan(logf, jnp.add, 0.0, forward)
    r = gates[0:nh, :] * LOG2_E - b
    cm = _lane_scan(r, jnp.maximum, -jnp.inf, forward)
    g_tot = jnp.broadcast_to(jnp.sum(logf, axis=1, keepdims=True), (nh, length))
    a = g_tot + r
    m_loc = jnp.broadcast_to(jnp.max(a, axis=1, keepdims=True), (nh, length))
    w_loc = jnp.exp2(a - m_loc)
    pieces = _bf16_pieces(-cm, 3) + _bf16_pieces(-b, 3) + _bf16_pieces(w_loc, 2)
    stacked = jnp.concatenate(pieces, axis=0).astype(BF16)
    return jnp.concatenate([r, g_tot, m_loc], axis=0), stacked


def _pieces_to_cols(stacked):
    nh = N_HEADS
    n_rows = stacked.shape[0]
    prow = lax.broadcasted_iota(jnp.int32, (n_rows, 4 * nh), 0)
    pcol = lax.broadcasted_iota(jnp.int32, (n_rows, 4 * nh), 1)
    quantity = (prow >= 3 * nh).astype(jnp.int32) + (prow >= 6 * nh).astype(jnp.int32)
    gather = (pcol == (prow % nh) + nh * quantity).astype(BF16)
    return lax.dot_general(stacked, gather, (((0,), (0,)), ((), ())), preferred_element_type=F32)


def _mlstm_in_body(x_ref, g_ref, win_ref, wgate_t_ref, bgate_t_ref,
                   q_ref, k_ref, v_ref, o_ref, rows_fw_ref, rows_bw_ref, cols_fw_ref, cols_bw_ref):
    d_qk = q_ref.shape[2]
    d_v = v_ref.shape[2]
    h = _rmsnorm(x_ref[0], g_ref[...]).astype(BF16)
    gates_t = lax.dot_general(wgate_t_ref[...], h, (((1,), (1,)), ((), ())),
                              preferred_element_type=F32) + bgate_t_ref[...]
    half = 2 * N_HEADS
    stacked = []
    for c in range(rows_fw_ref.shape[1]):
        sl = slice(c * SCAN_CHUNK, (c + 1) * SCAN_CHUNK)
        rows_fw_ref[0, c], st_fw = _chunk_gate_features(gates_t[0:half, sl], True)
        rows_bw_ref[0, c], st_bw = _chunk_gate_features(gates_t[half:, sl], False)
        stacked.append((sl, st_fw, st_bw))
    q_ref[0] = _dot(h, win_ref[:, 0:d_qk]).astype(BF16)
    k_ref[0] = (_dot(h, win_ref[:, d_qk:2 * d_qk]) * (DH_QK ** -0.5)).astype(BF16)
    v_ref[0] = _dot(h, win_ref[:, 2 * d_qk:2 * d_qk + d_v]).astype(BF16)
    o_ref[0] = _dot(h, win_ref[:, 2 * d_qk + d_v:])
    for sl, st_fw, st_bw in stacked:
        cols_fw_ref[0, sl, :] = _pieces_to_cols(st_fw)
        cols_bw_ref[0, sl, :] = _pieces_to_cols(st_bw)


def _mlstm_in(x, layer, j, g_norm, w_in, w_gate_t, b_gate_t):
    b, s, d = x.shape
    tm = WIDE_TILE
    d_qk = N_HEADS * DH_QK
    d_v = N_HEADS * DH_V
    n_gate = 4 * N_HEADS
    n_rows = 3 * N_HEADS
    cpt = tm // SCAN_CHUNK
    assert s % tm == 0 and tm % SCAN_CHUNK == 0
    tok = lambda width: pl.BlockSpec((1, tm, width), lambda bi, i: (bi, i, 0))
    rows_spec = pl.BlockSpec((1, cpt, n_rows, SCAN_CHUNK), lambda bi, i: (bi, i, 0, 0))
    rows_shape = jax.ShapeDtypeStruct((b, s // SCAN_CHUNK, n_rows, SCAN_CHUNK), F32)
    cols_shape = jax.ShapeDtypeStruct((b, s, n_gate), F32)
    est = d * (2 * d_qk + 2 * d_v) * 2 + 2 * tm * (d * 4 + (2 * d_qk + d_v) * 2 + d_v * 4) + 4 * tm * d * 4
    params = [_layer_param(g_norm, layer)] + [_layer_param(a, j) for a in (w_in, w_gate_t, b_gate_t)]
    return pl.pallas_call(
        _mlstm_in_body,
        grid=(b, s // tm),
        in_specs=[tok(d)] + [spec for spec, _ in params],
        out_specs=[tok(d_qk), tok(d_qk), tok(d_v), tok(d_v), rows_spec, rows_spec, tok(n_gate), tok(n_gate)],
        out_shape=[jax.ShapeDtypeStruct((b, s, d_qk), BF16), jax.ShapeDtypeStruct((b, s, d_qk), BF16),
                   jax.ShapeDtypeStruct((b, s, d_v), BF16), jax.ShapeDtypeStruct((b, s, d_v), F32),
                   rows_shape, rows_shape, cols_shape, cols_shape],
        compiler_params=pltpu.CompilerParams(
            dimension_semantics=("parallel", "parallel"), vmem_limit_bytes=_vmem_limit(est)),
        name="mlstm_in",
    )(x, *[a for _, a in params])


def _scan_chunk(forward, q, k, v, rows, cols, c_ref, m_ref, h_ref, row0):
    nh = N_HEADS
    length = q.shape[0]
    ti = lax.broadcasted_iota(jnp.int32, (length, length), 0)
    ji = lax.broadcasted_iota(jnp.int32, (length, length), 1)
    seen = (ji <= ti) if forward else (ji >= ti)

    r = rows[0:nh, :]
    g_tot = rows[nh:2 * nh, :]
    m_loc = rows[2 * nh:, :]
    m_prev = m_ref[...]
    m_new = jnp.maximum(g_tot + m_prev, m_loc)
    s_old = jnp.exp2(g_tot + m_prev - m_new)
    s_loc = jnp.exp2(m_loc - m_new)
    tn =(((0,), (0,)), ((), ()))
    nt = (((1,), (1,)), ((), ()))

    qk_tile = (length, 2 * DH_QK)
    low = lax.broadcasted_iota(jnp.int32, qk_tile, 1) < DH_QK
    ones_blk = jnp.ones((length, DH_V), BF16)

    def v_aug(hd):
        return jnp.concatenate([v[:, DH_V * hd:DH_V * (hd + 1)], ones_blk], axis=1)

    def scores(p):
        qp = q[:, 2 * DH_QK * p:2 * DH_QK * (p + 1)]
        kp = k[:, 2 * DH_QK * p:2 * DH_QK * (p + 1)]
        qms = [jnp.where(low, qp, jnp.zeros_like(qp)), jnp.where(low, jnp.zeros_like(qp), qp)]
        s2 = lax.dot_general(jnp.concatenate(qms, axis=0), kp, nt, preferred_element_type=F32)
        return qms, [s2[0:length], s2[length:]]

    def outputs(hd, qm, s):
        p = hd // 2
        mp = m_prev[hd:hd + 1, 0:V7X_LANES]
        neg_m = jnp.minimum(jnp.broadcast_to(cols[:, hd:hd + 1], (length, V7X_LANES)), -mp)
        neg_m_keys = jnp.concatenate([neg_m] * (length // V7X_LANES), axis=1)
        sc = s * jnp.exp2(jnp.where(seen, neg_m_keys + r[hd:hd + 1, :], -jnp.inf))
        wq = qm.astype(F32) * jnp.exp2(neg_m + mp)
        p_ext = jnp.concatenate([sc.astype(BF16), wq.astype(BF16)], axis=1)
        nd = _dot(p_ext, jnp.concatenate([v_aug(hd), c_ref[p].astype(BF16)], axis=0))
        neg_b = jnp.broadcast_to(cols[:, nh + hd:nh + hd + 1], (length, DH_V))
        floor = jnp.exp2(neg_b + neg_m)
        den = jnp.maximum(jnp.abs(nd[:, DH_V:]), floor)
        h_ref[0, pl.ds(row0, length), DH_V * hd:DH_V * (hd + 1)] = nd[:, 0:DH_V] / den

    def update_state(p):
        kf = k[:, 2 * DH_QK * p:2 * DH_QK * (p + 1)].astype(F32)
        w0 = jnp.broadcast_to(cols[:, 2 * nh + 2 * p:2 * nh + 2 * p + 1], qk_tile)
        w1 = jnp.broadcast_to(cols[:, 2 * nh + 2 * p + 1:2 * nh + 2 * p + 2], qk_tile)
        kw = jnp.concatenate([kf * jnp.where(low, w0, 0.0), kf * jnp.where(low, 0.0, w1)], axis=0)
        c_loc = lax.dot_general(kw.astype(BF16), jnp.concatenate([v_aug(2 * p), v_aug(2 * p + 1)], axis=0), tn,
                                preferred_element_type=F32)

        def per_row(scale):
            halves = [jnp.broadcast_to(scale[hd:hd + 1, 0:DH_V], (DH_QK, DH_V)) for hd in (2 * p, 2 * p + 1)]
            half = jnp.concatenate(halves, axis=0)
            return jnp.concatenate([half, half], axis=1)

        c_ref[p] = per_row(s_old) * c_ref[p] + per_row(s_loc) * c_loc

    for p in range(nh // 2):
        qms, ss = scores(p)
        outputs(2 * p, qms[0], ss[0])
        outputs(2 * p + 1, qms[1], ss[1])
        update_state(p)
    m_ref[...] = m_new


def _mlstm_scan_body(qf_ref, kf_ref, vf_ref, rowsf_ref, colsf_ref, qb_ref, kb_ref, vb_ref, rowsb_ref, colsb_ref,
                     hf_ref, hb_ref, cf_scr, cb_scr, mf_scr, mb_scr):
    @pl.when(pl.program_id(1) == 0)
    def _():
        cf_scr[...] = jnp.zeros_like(cf_scr)
        cb_scr[...] = jnp.zeros_like(cb_scr)
        mf_scr[...] = jnp.zeros_like(mf_scr)
        mb_scr[...] = jnp.zeros_like(mb_scr)

    length = SCAN_CHUNK
    n_chunks = qf_ref.shape[1] // length

    for c in range(n_chunks):
        rf = c * length
        _scan_chunk(True, qf_ref[0, pl.ds(rf, length), :], kf_ref[0, pl.ds(rf, length), :],
                    vf_ref[0, pl.ds(rf, length), :], rowsf_ref[0, c], colsf_ref[0, pl.ds(rf, length), :],
                    cf_scr, mf_scr, hf_ref, rf)
        cb = n_chunks - 1 - c
        rb = cb * length
        _scan_chunk(False, qb_ref[0, pl.ds(rb, length), :], kb_ref[0, pl.ds(rb, length), :],
                    vb_ref[0, pl.ds(rb, length), :], rowsb_ref[0, cb], colsb_ref[0, pl.ds(rb, length), :],
                    cb_scr, mb_scr, hb_ref, rb)


def _mlstm_scan(q, k, v, rows_fw, rows_bw, cols_fw, cols_bw):
    b, s, d_qk = q.shape
    d_v = v.shape[2]
    tb = SCAN_BLOCK
    nb = s // tb
    cpb = tb // SCAN_CHUNK
    n_rows = rows_fw.shape[2]
    n_cols = cols_fw.shape[2]
    assert s % tb == 0 and tb % SCAN_CHUNK == 0
    assert SCAN_CHUNK % V7X_LANES == 0 and 2 * DH_QK == V7X_LANES and DH_V == V7X_LANES
    fw = lambda width: pl.BlockSpec((1, tb, width), lambda bi, i: (bi, i, 0))
    bw = lambda width: pl.BlockSpec((1, tb, width), lambda bi, i: (bi, nb - 1 - i, 0))
    rows_f = pl.BlockSpec((1, cpb, n_rows, SCAN_CHUNK), lambda bi, i: (bi, i, 0, 0))
    rows_b = pl.BlockSpec((1, cpb, n_rows, SCAN_CHUNK), lambda bi, i: (bi, nb - 1 - i, 0, 0))
    state = pltpu.VMEM((N_HEADS // 2, 2 * DH_QK, 2 * DH_V), F32)
    stab = pltpu.VMEM((N_HEADS, SCAN_CHUNK), F32)
    est = 2 * 2 * tb * ((2 * d_qk + d_v) * 2 + d_v * 4 + (n_rows + V7X_LANES) * 4) + 16 * SCAN_CHUNK * 1024 * 4
    return pl.pallas_call(
        _mlstm_scan_body,
        grid=(b, nb),
        in_specs=[fw(d_qk), fw(d_qk), fw(d_v), rows_f, fw(n_cols),
                  bw(d_qk), bw(d_qk), bw(d_v), rows_b, bw(n_cols)],
        out_specs=[fw(d_v), bw(d_v)],
        out_shape=[jax.ShapeDtypeStruct((b, s, d_v), F32), jax.ShapeDtypeStruct((b, s, d_v), F32)],
        scratch_shapes=[state, state, stab, stab],
        compiler_params=pltpu.CompilerParams(
            dimension_semantics=("arbitrary", "arbitrary"), vmem_limit_bytes=_vmem_limit(est)),
        name="mlstm_scan",
    )(q, k, v, rows_fw, cols_fw, q, k, v, rows_bw, cols_bw)


def _trunk(x, p):
    b, s, d = x.shape
    flat = lambda a: a.reshape(b * s, a.shape[-1])
    depth = p["g_ffn1"].shape[0]
    for layer in range(depth):
        j = layer // 2
        x = _ffn(flat(x), layer, p["g_ffn1"], p["w_ffn1_gate"], p["w_ffn1_up"], p["w_ffn1_down"]).reshape(b, s, d)
        mlstm = None
        if layer % 2 == 0:
            x = _conv_mixer(x, layer, j, p["g_mix"], p["w_conv_in"], p["w_conv_dw"], p["w_conv_out"])
        else:
            q, k, v, o, rows_fw, rows_bw, cols_fw, cols_bw = _mlstm_in(
                x, layer, j, p["g_mix"], p["w_mlstm_in"], p["w_mlstm_gate_t"], p["b_mlstm_gate_t"])
            h_fw, h_bw = _mlstm_scan(q, k, v, rows_fw, rows_bw, cols_fw, cols_bw)
            mlstm = (flat(h_fw), flat(h_bw), flat(o), p["g_mlstm_head"], p["w_mlstm_out"], j)
        g_final = p["g_final"] if layer == depth - 1 else None
        x = _ffn(flat(x), layer, p["g_ffn2"], p["w_ffn2_gate"], p["w_ffn2_up"], p["w_ffn2_down"],
                 mlstm, g_final).reshape(b, s, d)
    return x


def kernel(x_prompt, x_sample, g_ffn1, w_ffn1_gate, w_ffn1_up, w_ffn1_down, g_mix, w_conv_in, w_conv_dw, w_conv_out, w_mlstm_in, w_mlstm_gate, b_mlstm_gate, g_mlstm_head, w_mlstm_out, g_ffn2, w_ffn2_gate, w_ffn2_up, w_ffn2_down, g_final):
    bf = lambda w: w.astype(BF16)
    p = dict(
        g_ffn1=g_ffn1, w_ffn1_gate=bf(w_ffn1_gate), w_ffn1_up=bf(w_ffn1_up), w_ffn1_down=bf(w_ffn1_down),
        g_mix=g_mix, w_conv_in=bf(w_conv_in), w_conv_dw=w_conv_dw, w_conv_out=bf(w_conv_out),
        w_mlstm_in=bf(w_mlstm_in), w_mlstm_gate_t=bf(jnp.swapaxes(w_mlstm_gate, 1, 2)),
        b_mlstm_gate_t=b_mlstm_gate[:, :, None],
        g_mlstm_head=g_mlstm_head, w_mlstm_out=bf(w_mlstm_out),
        g_ffn2=g_ffn2, w_ffn2_gate=bf(w_ffn2_gate), w_ffn2_up=bf(w_ffn2_up), w_ffn2_down=bf(w_ffn2_down),
        g_final=g_final)
    return _trunk(x_prompt, p), _trunk(x_sample, p)
```

```python
import functools

import jax
import jax.numpy as jnp
from jax import lax
from jax.experimental import pallas as pl
from jax.experimental.pallas import tpu as pltpu

F32 = jnp.float32
BF16 = jnp.bfloat16

N_HEADS = 8
DH_QK = 64
DH_V = 128
EPS = 1e-6
LOG2_E = 1.4426950408889634

V7X_VMEM_BYTES = 64 * 1024 * 1024
V7X_LANES = 128
BF16_SUBLANES = 16

TOKEN_TILE = 512
WIDE_TILE = 1024
FFN_COL_CHUNK = 256
CONV_COL_CHUNK = 256
SCAN_CHUNK = 256
SCAN_BLOCK = 512
CONV_HALO = BF16_SUBLANES


def _vmem_limit(estimate_bytes):
    return int(min(max(2 * estimate_bytes, 32 * 1024 * 1024), V7X_VMEM_BYTES - 8 * 1024 * 1024))


def _resident(shape):
    zeros = (0,) * len(shape)
    return pl.BlockSpec(shape, lambda *_: zeros, pipeline_mode=pl.Buffered(1))


def _layer_param(stack, layer):
    if stack.ndim == 2:
        stack = stack.reshape(stack.shape[0], 1, stack.shape[1])
    zeros = (0,) * (stack.ndim - 1)
    spec = pl.BlockSpec((None,) + stack.shape[1:], lambda *_: (layer,) + zeros, pipeline_mode=pl.Buffered(1))
    return spec, stack


def _rmsnorm(x, g):
    ms = jnp.mean(x * x, axis=-1, keepdims=True)
    return x * lax.rsqrt(ms + EPS) * g


def _dot(a, b):
    return jnp.dot(a, b, preferred_element_type=F32)


def _mlstm_output(h, o, g_head, w_out):
    parts = []
    for hd in range(N_HEADS):
        hh = h[:, DH_V * hd:DH_V * (hd + 1)]
        parts.append(hh * lax.rsqrt(jnp.mean(hh * hh, axis=-1, keepdims=True) + EPS))
    hn = jnp.concatenate(parts, axis=1) * g_head
    return _dot((hn * jax.nn.sigmoid(o)).astype(BF16), w_out)


def _ffn_body(*refs, mlstm_pre, final_norm):
    refs = list(refs)
    x_ref, gn_ref, wg_ref, wu_ref, wd_ref = refs[:5]
    o_ref, h_scr = refs[-2:]
    extra = refs[5:-2]
    if mlstm_pre:
        hf_ref, hb_ref, og_ref, gh_ref, wo_ref = extra[:5]
    if final_norm:
        gf_ref = extra[-1]
    d_ff = wg_ref.shape[1]
    x = x_ref[...]
    if mlstm_pre:
        x = x + _mlstm_output(hf_ref[...] + hb_ref[...], og_ref[...], gh_ref[...], wo_ref[...])
    xn = _rmsnorm(x, gn_ref[...]).astype(BF16)
    for c in range(d_ff // FFN_COL_CHUNK):
        sl = slice(c * FFN_COL_CHUNK, (c + 1) * FFN_COL_CHUNK)
        gate = _dot(xn, wg_ref[:, sl])
        up = _dot(xn, wu_ref[:, sl])
        h_scr[:, sl] = (gate * jax.nn.sigmoid(gate) * up).astype(BF16)
    out = x + 0.5 * _dot(h_scr[...], wd_ref[...])
    if final_norm:
        out = _rmsnorm(out, gf_ref[...])
    o_ref[...] = out


def _ffn(x2d, layer, g_norm, w_gate, w_up, w_down, mlstm=None, g_final=None):
    t, d = x2d.shape
    d_ff = w_gate.shape[2]
    tm = TOKEN_TILE if mlstm is not None else WIDE_TILE
    assert t % tm == 0 and d_ff % FFN_COL_CHUNK == 0
    final_norm = g_final is not None
    row = pl.BlockSpec((tm, d), lambda i: (i, 0))
    params = [_layer_param(a, layer) for a in (g_norm, w_gate, w_up, w_down)]
    in_specs = [row] + [spec for spec, _ in params]
    args = [x2d] + [a for _, a in params]
    est = 3 * d * d_ff * 2 + 4 * tm * d * 4 + tm * d_ff * 2 + 4 * tm * d * 4
    if mlstm is not None:
        h_fw, h_bw, o, g_head, w_out, j = mlstm
        d_v = h_fw.shape[1]
        row_v = pl.BlockSpec((tm, d_v), lambda i: (i, 0))
        params = [_layer_param(a, j) for a in (g_head, w_out)]
        in_specs += [row_v, row_v, row_v] + [spec for spec, _ in params]
        args += [h_fw, h_bw, o] + [a for _, a in params]
        est += d_v * d * 2 + 8 * tm * d_v * 4
    if final_norm:
        in_specs.append(_resident((1, d)))
        args.append(g_final.reshape(1, d))
    return pl.pallas_call(
        functools.partial(_ffn_body, mlstm_pre=mlstm is not None, final_norm=final_norm),
        grid=(t // tm,),
        in_specs=in_specs,
        out_specs=row,
        out_shape=jax.ShapeDtypeStruct((t, d), F32),
        scratch_shapes=[pltpu.VMEM((tm, d_ff), BF16)],
        compiler_params=pltpu.CompilerParams(
            dimension_semantics=("parallel",), vmem_limit_bytes=_vmem_limit(est)),
        name="ffn",
    )(*args)


def _conv_body(xm_ref, xp_ref, xn_ref, g_ref, win_ref, wdw_ref, wout_ref, o_ref, h_scr, u_scr):
    i = pl.program_id(1)
    n = pl.num_programs(1)
    tm = xm_ref.shape[1]
    d = xm_ref.shape[2]
    halo = CONV_HALO
    rows = tm + 2 * halo
    g = g_ref[...]
    xm = xm_ref[0]
    h_scr[0:halo] = _rmsnorm(xp_ref[0], g).astype(BF16)
    h_scr[halo:halo + tm] = _rmsnorm(xm, g).astype(BF16)
    h_scr[halo + tm:] = _rmsnorm(xn_ref[0], g).astype(BF16)
    h = h_scr[...]
    row = lax.broadcasted_iota(jnp.int32, (rows, 1), 0)
    inside = jnp.logical_and(jnp.logical_or(row >= halo, i > 0),
                             jnp.logical_or(row < halo + tm, i < n - 1))
    for c in range(d // CONV_COL_CHUNK):
        cs = slice(c * CONV_COL_CHUNK, (c + 1) * CONV_COL_CHUNK)
        c_cols = slice(d + c * CONV_COL_CHUNK, d + (c + 1) * CONV_COL_CHUNK)
        h_cols = slice(2 * d + c * CONV_COL_CHUNK, 2 * d + (c + 1) * CONV_COL_CHUNK)
        z = jnp.where(inside, _dot(h, win_ref[:, c_cols]) * _dot(h, win_ref[:, h_cols]), 0.0)
        z_prev = pltpu.roll(z, 1, axis=0)[halo:halo + tm]
        z_next = pltpu.roll(z, rows - 1, axis=0)[halo:halo + tm]
        y = wdw_ref[0:1, cs] * z_prev + wdw_ref[1:2, cs] * z[halo:halo + tm] + wdw_ref[2:3, cs] * z_next
        b_gate = _dot(h_scr[halo:halo + tm], win_ref[:, cs])
        u_scr[:, cs] = (b_gate * y).astype(BF16)
    o_ref[0] = xm + _dot(u_scr[...], wout_ref[...])


def _conv_mixer(x, layer, j, g_norm, w_in, w_dw, w_out):
    b, s, d = x.shape
    tm = WIDE_TILE
    halo = CONV_HALO
    assert s % tm == 0 and tm % halo == 0 and d % CONV_COL_CHUNK == 0
    per = tm // halo
    last = s // halo - 1
    main = pl.BlockSpec((1, tm, d), lambda bi, i: (bi, i, 0))
    prev = pl.BlockSpec((1, halo, d), lambda bi, i: (bi, jnp.maximum(i * per - 1, 0), 0))
    nxt = pl.BlockSpec((1, halo, d), lambda bi, i: (bi, jnp.minimum((i + 1) * per, last), 0))
    est = 4 * d * d * 2 + 4 * tm * d * 4 + 2 * (tm + 2 * halo) * d * 2 + 8 * (tm + 2 * halo) * CONV_COL_CHUNK * 4
    params = [_layer_param(g_norm, layer)] + [_layer_param(a, j) for a in (w_in, w_dw, w_out)]
    return pl.pallas_call(
        _conv_body,
        grid=(b, s // tm),
        in_specs=[main, prev, nxt] + [spec for spec, _ in params],
        out_specs=main,
        out_shape=jax.ShapeDtypeStruct((b, s, d), F32),
        scratch_shapes=[pltpu.VMEM((tm + 2 * halo, d), BF16), pltpu.VMEM((tm, d), BF16)],
        compiler_params=pltpu.CompilerParams(
            dimension_semantics=("parallel", "parallel"), vmem_limit_bytes=_vmem_limit(est)),
        name="conv_mixer",
    )(x, x, x, *[a for _, a in params])


def _log_sigmoid(x):
    return jnp.minimum(x, 0.0) - jnp.log1p(jnp.exp(-jnp.abs(x)))


def _lane_scan(x, op, identity, forward):
    length = x.shape[1]
    lane = lax.broadcasted_iota(jnp.int32, x.shape, 1)
    shift = 1
    while shift < length:
        if forward:
            moved = pltpu.roll(x, shift, axis=1)
            valid = lane >= shift
        else:
            moved = pltpu.roll(x, length - shift, axis=1)
            valid = lane < length - shift
        x = op(x, jnp.where(valid, moved, identity))
        shift *= 2
    return x


def _bf16_pieces(x, n):
    pieces = []
    for _ in range(n):
        piece = x.astype(BF16).astype(F32)
        pieces.append(piece)
        x = x - piece
    return pieces


def _chunk_gate_features(gates, forward):
    nh = N_HEADS
    length = gates.shape[1]
    logf = _log_sigmoid(gates[nh:, :]) * LOG2_E
    b = _lane_scan(logf, jnp.add, 0.0, forward)
    r = gates[0:nh, :] * LOG2_E - b
    cm = _lane_scan(r, jnp.maximum, -jnp.inf, forward)
    g_tot = jnp.broadcast_to(jnp.sum(logf, axis=1, keepdims=True), (nh, length))
    a = g_tot + r
    m_loc = jnp.broadcast_to(jnp.max(a, axis=1, keepdims=True), (nh, length))
    w_loc = jnp.exp2(a - m_loc)
    pieces = _bf16_pieces(-cm, 3) + _bf16_pieces(-b, 3) + _bf16_pieces(w_loc, 2)
    stacked = jnp.concatenate(pieces, axis=0).astype(BF16)
    return jnp.concatenate([r, g_tot, m_loc], axis=0), stacked


def _pieces_to_cols(stacked):
    nh = N_HEADS
    n_rows = stacked.shape[0]
    prow = lax.broadcasted_iota(jnp.int32, (n_rows, 4 * nh), 0)
    pcol = lax.broadcasted_iota(jnp.int32, (n_rows, 4 * nh), 1)
    quantity = (prow >= 3 * nh).astype(jnp.int32) + (prow >= 6 * nh).astype(jnp.int32)
    gather = (pcol == (prow % nh) + nh * quantity).astype(BF16)
    return lax.dot_general(stacked, gather, (((0,), (0,)), ((), ())), preferred_element_type=F32)


def _mlstm_in_body(x_ref, g_ref, win_ref, wgate_t_ref, bgate_t_ref,
                   q_ref, k_ref, v_ref, o_ref, rows_fw_ref, rows_bw_ref, cols_fw_ref, cols_bw_ref):
    d_qk = q_ref.shape[2]
    d_v = v_ref.shape[2]
    h = _rmsnorm(x_ref[0], g_ref[...]).astype(BF16)
    gates_t = lax.dot_general(wgate_t_ref[...], h, (((1,), (1,)), ((), ())),
                              preferred_element_type=F32) + bgate_t_ref[...]
    half = 2 * N_HEADS
    stacked = []
    for c in range(rows_fw_ref.shape[1]):
        sl = slice(c * SCAN_CHUNK, (c + 1) * SCAN_CHUNK)
        rows_fw_ref[0, c], st_fw = _chunk_gate_features(gates_t[0:half, sl], True)
        rows_bw_ref[0, c], st_bw = _chunk_gate_features(gates_t[half:, sl], False)
        stacked.append((sl, st_fw, st_bw))
    q_ref[0] = _dot(h, win_ref[:, 0:d_qk]).astype(BF16)
    k_ref[0] = (_dot(h, win_ref[:, d_qk:2 * d_qk]) * (DH_QK ** -0.5)).astype(BF16)
    v_ref[0] = _dot(h, win_ref[:, 2 * d_qk:2 * d_qk + d_v]).astype(BF16)
    o_ref[0] = _dot(h, win_ref[:, 2 * d_qk + d_v:])
    for sl, st_fw, st_bw in stacked:
        cols_fw_ref[0, sl, :] = _pieces_to_cols(st_fw)
        cols_bw_ref[0, sl, :] = _pieces_to_cols(st_bw)


def _mlstm_in(x, layer, j, g_norm, w_in, w_gate_t, b_gate_t):
    b, s, d = x.shape
    tm = WIDE_TILE
    d_qk = N_HEADS * DH_QK
    d_v = N_HEADS * DH_V
    n_gate = 4 * N_HEADS
    n_rows = 3 * N_HEADS
    cpt = tm // SCAN_CHUNK
    assert s % tm == 0 and tm % SCAN_CHUNK == 0
    tok = lambda width: pl.BlockSpec((1, tm, width), lambda bi, i: (bi, i, 0))
    rows_spec = pl.BlockSpec((1, cpt, n_rows, SCAN_CHUNK), lambda bi, i: (bi, i, 0, 0))
    rows_shape = jax.ShapeDtypeStruct((b, s // SCAN_CHUNK, n_rows, SCAN_CHUNK), F32)
    cols_shape = jax.ShapeDtypeStruct((b, s, n_gate), F32)
    est = d * (2 * d_qk + 2 * d_v) * 2 + 2 * tm * (d * 4 + (2 * d_qk + d_v) * 2 + d_v * 4) + 4 * tm * d * 4
    params = [_layer_param(g_norm, layer)] + [_layer_param(a, j) for a in (w_in, w_gate_t, b_gate_t)]
    return pl.pallas_call(
        _mlstm_in_body,
        grid=(b, s // tm),
        in_specs=[tok(d)] + [spec for spec, _ in params],
        out_specs=[tok(d_qk), tok(d_qk), tok(d_v), tok(d_v), rows_spec, rows_spec, tok(n_gate), tok(n_gate)],
        out_shape=[jax.ShapeDtypeStruct((b, s, d_qk), BF16), jax.ShapeDtypeStruct((b, s, d_qk), BF16),
                   jax.ShapeDtypeStruct((b, s, d_v), BF16), jax.ShapeDtypeStruct((b, s, d_v), F32),
                   rows_shape, rows_shape, cols_shape, cols_shape],
        compiler_params=pltpu.CompilerParams(
            dimension_semantics=("parallel", "parallel"), vmem_limit_bytes=_vmem_limit(est)),
        name="mlstm_in",
    )(x, *[a for _, a in params])


def _scan_chunk(forward, q, k, v, rows, cols, c_ref, m_ref, h_ref, row0):
    nh = N_HEADS
    length = q.shape[0]
    ti = lax.broadcasted_iota(jnp.int32, (length, length), 0)
    ji = lax.broadcasted_iota(jnp.int32, (length, length), 1)
    seen = (ji <= ti) if forward else (ji >= ti)

    r = rows[0:nh, :]
    g_tot = rows[nh:2 * nh, :]
    m_loc = rows[2 * nh:, :]
    m_prev = m_ref[...]
    m_new = jnp.maximum(g_tot + m_prev, m_loc)
    s_old = jnp.exp2(g_tot + m_prev - m_new)
    s_loc = jnp.exp2(m_loc - m_new)
    tn =(((0,), (0,)), ((), ()))
    nt = (((1,), (1,)), ((), ()))

    qk_tile = (length, 2 * DH_QK)
    low = lax.broadcasted_iota(jnp.int32, qk_tile, 1) < DH_QK
    ones_blk = jnp.ones((length, DH_V), BF16)

    def v_aug(hd):
        return jnp.concatenate([v[:, DH_V * hd:DH_V * (hd + 1)], ones_blk], axis=1)

    def scores(p):
        qp = q[:, 2 * DH_QK * p:2 * DH_QK * (p + 1)]
        kp = k[:, 2 * DH_QK * p:2 * DH_QK * (p + 1)]
        qms = [jnp.where(low, qp, jnp.zeros_like(qp)), jnp.where(low, jnp.zeros_like(qp), qp)]
        s2 = lax.dot_general(jnp.concatenate(qms, axis=0), kp, nt, preferred_element_type=F32)
        return qms, [s2[0:length], s2[length:]]

    def outputs(hd, qm, s):
        p = hd // 2
        mp = m_prev[hd:hd + 1, 0:V7X_LANES]
        neg_m = jnp.minimum(jnp.broadcast_to(cols[:, hd:hd + 1], (length, V7X_LANES)), -mp)
        neg_m_keys = jnp.concatenate([neg_m] * (length // V7X_LANES), axis=1)
        sc = s * jnp.exp2(jnp.where(seen, neg_m_keys + r[hd:hd + 1, :], -jnp.inf))
        wq = qm.astype(F32) * jnp.exp2(neg_m + mp)
        p_ext = jnp.concatenate([sc.astype(BF16), wq.astype(BF16)], axis=1)
        nd = _dot(p_ext, jnp.concatenate([v_aug(hd), c_ref[p].astype(BF16)], axis=0))
        neg_b = jnp.broadcast_to(cols[:, nh + hd:nh + hd + 1], (length, DH_V))
        floor = jnp.exp2(neg_b + neg_m)
        den = jnp.maximum(jnp.abs(nd[:, DH_V:]), floor)
        h_ref[0, pl.ds(row0, length), DH_V * hd:DH_V * (hd + 1)] = nd[:, 0:DH_V] / den

    def update_state(p):
        kf = k[:, 2 * DH_QK * p:2 * DH_QK * (p + 1)].astype(F32)
        w0 = jnp.broadcast_to(cols[:, 2 * nh + 2 * p:2 * nh + 2 * p + 1], qk_tile)
        w1 = jnp.broadcast_to(cols[:, 2 * nh + 2 * p + 1:2 * nh + 2 * p + 2], qk_tile)
        kw = jnp.concatenate([kf * jnp.where(low, w0, 0.0), kf * jnp.where(low, 0.0, w1)], axis=0)
        c_loc = lax.dot_general(kw.astype(BF16), jnp.concatenate([v_aug(2 * p), v_aug(2 * p + 1)], axis=0), tn,
                                preferred_element_type=F32)

        def per_row(scale):
            halves = [jnp.broadcast_to(scale[hd:hd + 1, 0:DH_V], (DH_QK, DH_V)) for hd in (2 * p, 2 * p + 1)]
            half = jnp.concatenate(halves, axis=0)
            return jnp.concatenate([half, half], axis=1)

        c_ref[p] = per_row(s_old) * c_ref[p] + per_row(s_loc) * c_loc

    for p in range(nh // 2):
        qms, ss = scores(p)
        outputs(2 * p, qms[0], ss[0])
        outputs(2 * p + 1, qms[1], ss[1])
        update_state(p)
    m_ref[...] = m_new


def _mlstm_scan_body(qf_ref, kf_ref, vf_ref, rowsf_ref, colsf_ref, qb_ref, kb_ref, vb_ref, rowsb_ref, colsb_ref,
                     hf_ref, hb_ref, cf_scr, cb_scr, mf_scr, mb_scr):
    @pl.when(pl.program_id(1) == 0)
    def _():
        cf_scr[...] = jnp.zeros_like(cf_scr)
        cb_scr[...] = jnp.zeros_like(cb_scr)
        mf_scr[...] = jnp.zeros_like(mf_scr)
        mb_scr[...] = jnp.zeros_like(mb_scr)

    length = SCAN_CHUNK
    n_chunks = qf_ref.shape[1] // length

    for c in range(n_chunks):
        rf = c * length
        _scan_chunk(True, qf_ref[0, pl.ds(rf, length), :], kf_ref[0, pl.ds(rf, length), :],
                    vf_ref[0, pl.ds(rf, length), :], rowsf_ref[0, c], colsf_ref[0, pl.ds(rf, length), :],
                    cf_scr, mf_scr, hf_ref, rf)
        cb = n_chunks - 1 - c
        rb = cb * length
        _scan_chunk(False, qb_ref[0, pl.ds(rb, length), :], kb_ref[0, pl.ds(rb, length), :],
                    vb_ref[0, pl.ds(rb, length), :], rowsb_ref[0, cb], colsb_ref[0, pl.ds(rb, length), :],
                    cb_scr, mb_scr, hb_ref, rb)


def _mlstm_scan(q, k, v, rows_fw, rows_bw, cols_fw, cols_bw):
    b, s, d_qk = q.shape
    d_v = v.shape[2]
    tb = SCAN_BLOCK
    nb = s // tb
    cpb = tb // SCAN_CHUNK
    n_rows = rows_fw.shape[2]
    n_cols = cols_fw.shape[2]
    assert s % tb == 0 and tb % SCAN_CHUNK == 0
    assert SCAN_CHUNK % V7X_LANES == 0 and 2 * DH_QK == V7X_LANES and DH_V == V7X_LANES
    fw = lambda width: pl.BlockSpec((1, tb, width), lambda bi, i: (bi, i, 0))
    bw = lambda width: pl.BlockSpec((1, tb, width), lambda bi, i: (bi, nb - 1 - i, 0))
    rows_f = pl.BlockSpec((1, cpb, n_rows, SCAN_CHUNK), lambda bi, i: (bi, i, 0, 0))
    rows_b = pl.BlockSpec((1, cpb, n_rows, SCAN_CHUNK), lambda bi, i: (bi, nb - 1 - i, 0, 0))
    state = pltpu.VMEM((N_HEADS // 2, 2 * DH_QK, 2 * DH_V), F32)
    stab = pltpu.VMEM((N_HEADS, SCAN_CHUNK), F32)
    est = 2 * 2 * tb * ((2 * d_qk + d_v) * 2 + d_v * 4 + (n_rows + V7X_LANES) * 4) + 16 * SCAN_CHUNK * 1024 * 4
    return pl.pallas_call(
        _mlstm_scan_body,
        grid=(b, nb),
        in_specs=[fw(d_qk), fw(d_qk), fw(d_v), rows_f, fw(n_cols),
                  bw(d_qk), bw(d_qk), bw(d_v), rows_b, bw(n_cols)],
        out_specs=[fw(d_v), bw(d_v)],
        out_shape=[jax.ShapeDtypeStruct((b, s, d_v), F32), jax.ShapeDtypeStruct((b, s, d_v), F32)],
        scratch_shapes=[state, state, stab, stab],
        compiler_params=pltpu.CompilerParams(
            dimension_semantics=("arbitrary", "arbitrary"), vmem_limit_bytes=_vmem_limit(est)),
        name="mlstm_scan",
    )(q, k, v, rows_fw, cols_fw, q, k, v, rows_bw, cols_bw)


def _trunk(x, p):
    b, s, d = x.shape
    flat = lambda a: a.reshape(b * s, a.shape[-1])
    depth = p["g_ffn1"].shape[0]
    for layer in range(depth):
        j = layer // 2
        x = _ffn(flat(x), layer, p["g_ffn1"], p["w_ffn1_gate"], p["w_ffn1_up"], p["w_ffn1_down"]).reshape(b, s, d)
        mlstm = None
        if layer % 2 == 0:
            x = _conv_mixer(x, layer, j, p["g_mix"], p["w_conv_in"], p["w_conv_dw"], p["w_conv_out"])
        else:
            q, k, v, o, rows_fw, rows_bw, cols_fw, cols_bw = _mlstm_in(
                x, layer, j, p["g_mix"], p["w_mlstm_in"], p["w_mlstm_gate_t"], p["b_mlstm_gate_t"])
            h_fw, h_bw = _mlstm_scan(q, k, v, rows_fw, rows_bw, cols_fw, cols_bw)
            mlstm = (flat(h_fw), flat(h_bw), flat(o), p["g_mlstm_head"], p["w_mlstm_out"], j)
        g_final = p["g_final"] if layer == depth - 1 else None
        x = _ffn(flat(x), layer, p["g_ffn2"], p["w_ffn2_gate"], p["w_ffn2_up"], p["w_ffn2_down"],
                 mlstm, g_final).reshape(b, s, d)
    return x


def kernel(x_prompt, x_sample, g_ffn1, w_ffn1_gate, w_ffn1_up, w_ffn1_down, g_mix, w_conv_in, w_conv_dw, w_conv_out, w_mlstm_in, w_mlstm_gate, b_mlstm_gate, g_mlstm_head, w_mlstm_out, g_ffn2, w_ffn2_gate, w_ffn2_up, w_ffn2_down, g_final):
    bf = lambda w: w.astype(BF16)
    p = dict(
        g_ffn1=g_ffn1, w_ffn1_gate=bf(w_ffn1_gate), w_ffn1_up=bf(w_ffn1_up), w_ffn1_down=bf(w_ffn1_down),
        g_mix=g_mix, w_conv_in=bf(w_conv_in), w_conv_dw=w_conv_dw, w_conv_out=bf(w_conv_out),
        w_mlstm_in=bf(w_mlstm_in), w_mlstm_gate_t=bf(jnp.swapaxes(w_mlstm_gate, 1, 2)),
        b_mlstm_gate_t=b_mlstm_gate[:, :, None],
        g_mlstm_head=g_mlstm_head, w_mlstm_out=bf(w_mlstm_out),
        g_ffn2=g_ffn2, w_ffn2_gate=bf(w_ffn2_gate), w_ffn2_up=bf(w_ffn2_up), w_ffn2_down=bf(w_ffn2_down),
        g_final=g_final)
    return _trunk(x_prompt, p), _trunk(x_sample, p)
```

```python
import functools

import jax
import jax.numpy as jnp
from jax import lax
from jax.experimental import pallas as pl
from jax.experimental.pallas import tpu as pltpu

F32 = jnp.float32
BF16 = jnp.bfloat16

N_HEADS = 8
DH_QK = 64
DH_V = 128
EPS = 1e-6
LOG2_E = 1.4426950408889634

V7X_VMEM_BYTES = 64 * 1024 * 1024
V7X_LANES = 128
BF16_SUBLANES = 16

TOKEN_TILE = 512
WIDE_TILE = 1024
FFN_COL_CHUNK = 256
CONV_COL_CHUNK = 256
SCAN_CHUNK = 256
SCAN_BLOCK = 512
CONV_HALO = BF16_SUBLANES


def _vmem_limit(estimate_bytes):
    return int(min(max(2 * estimate_bytes, 32 * 1024 * 1024), V7X_VMEM_BYTES - 8 * 1024 * 1024))


def _resident(shape):
    zeros = (0,) * len(shape)
    return pl.BlockSpec(shape, lambda *_: zeros, pipeline_mode=pl.Buffered(1))


def _layer_param(stack, layer):
    if stack.ndim == 2:
        stack = stack.reshape(stack.shape[0], 1, stack.shape[1])
    zeros = (0,) * (stack.ndim - 1)
    spec = pl.BlockSpec((None,) + stack.shape[1:], lambda *_: (layer,) + zeros, pipeline_mode=pl.Buffered(1))
    return spec, stack


def _rmsnorm(x, g):
    ms = jnp.mean(x * x, axis=-1, keepdims=True)
    return x * lax.rsqrt(ms + EPS) * g


def _dot(a, b):
    return jnp.dot(a, b, preferred_element_type=F32)


def _mlstm_output(h, o, g_head, w_out):
    parts = []
    for hd in range(N_HEADS):
        hh = h[:, DH_V * hd:DH_V * (hd + 1)]
        parts.append(hh * lax.rsqrt(jnp.mean(hh * hh, axis=-1, keepdims=True) + EPS))
    hn = jnp.concatenate(parts, axis=1) * g_head
    return _dot((hn * jax.nn.sigmoid(o)).astype(BF16), w_out)


def _ffn_body(*refs, mlstm_pre, final_norm):
    refs = list(refs)
    x_ref, gn_ref, wg_ref, wu_ref, wd_ref = refs[:5]
    o_ref, h_scr = refs[-2:]
    extra = refs[5:-2]
    if mlstm_pre:
        hf_ref, hb_ref, og_ref, gh_ref, wo_ref = extra[:5]
    if final_norm:
        gf_ref = extra[-1]
    d_ff = wg_ref.shape[1]
    x = x_ref[...]
    if mlstm_pre:
        x = x + _mlstm_output(hf_ref[...] + hb_ref[...], og_ref[...], gh_ref[...], wo_ref[...])
    xn = _rmsnorm(x, gn_ref[...]).astype(BF16)
    for c in range(d_ff // FFN_COL_CHUNK):
        sl = slice(c * FFN_COL_CHUNK, (c + 1) * FFN_COL_CHUNK)
        gate = _dot(xn, wg_ref[:, sl])
        up = _dot(xn, wu_ref[:, sl])
        h_scr[:, sl] = (gate * jax.nn.sigmoid(gate) * up).astype(BF16)
    out = x + 0.5 * _dot(h_scr[...], wd_ref[...])
    if final_norm:
        out = _rmsnorm(out, gf_ref[...])
    o_ref[...] = out


def _ffn(x2d, layer, g_norm, w_gate, w_up, w_down, mlstm=None, g_final=None):
    t, d = x2d.shape
    d_ff = w_gate.shape[2]
    tm = TOKEN_TILE if mlstm is not None else WIDE_TILE
    assert t % tm == 0 and d_ff % FFN_COL_CHUNK == 0
    final_norm = g_final is not None
    row = pl.BlockSpec((tm, d), lambda i: (i, 0))
    params = [_layer_param(a, layer) for a in (g_norm, w_gate, w_up, w_down)]
    in_specs = [row] + [spec for spec, _ in params]
    args = [x2d] + [a for _, a in params]
    est = 3 * d * d_ff * 2 + 4 * tm * d * 4 + tm * d_ff * 2 + 4 * tm * d * 4
    if mlstm is not None:
        h_fw, h_bw, o, g_head, w_out, j = mlstm
        d_v = h_fw.shape[1]
        row_v = pl.BlockSpec((tm, d_v), lambda i: (i, 0))
        params = [_layer_param(a, j) for a in (g_head, w_out)]
        in_specs += [row_v, row_v, row_v] + [spec for spec, _ in params]
        args += [h_fw, h_bw, o] + [a for _, a in params]
        est += d_v * d * 2 + 8 * tm * d_v * 4
    if final_norm:
        in_specs.append(_resident((1, d)))
        args.append(g_final.reshape(1, d))
    return pl.pallas_call(
        functools.partial(_ffn_body, mlstm_pre=mlstm is not None, final_norm=final_norm),
        grid=(t // tm,),
        in_specs=in_specs,
        out_specs=row,
        out_shape=jax.ShapeDtypeStruct((t, d), F32),
        scratch_shapes=[pltpu.VMEM((tm, d_ff), BF16)],
        compiler_params=pltpu.CompilerParams(
            dimension_semantics=("parallel",), vmem_limit_bytes=_vmem_limit(est)),
        name="ffn",
    )(*args)


def _conv_body(xm_ref, xp_ref, xn_ref, g_ref, win_ref, wdw_ref, wout_ref, o_ref, h_scr, u_scr):
    i = pl.program_id(1)
    n = pl.num_programs(1)
    tm = xm_ref.shape[1]
    d = xm_ref.shape[2]
    halo = CONV_HALO
    rows = tm + 2 * halo
    g = g_ref[...]
    xm = xm_ref[0]
    h_scr[0:halo] = _rmsnorm(xp_ref[0], g).astype(BF16)
    h_scr[halo:halo + tm] = _rmsnorm(xm, g).astype(BF16)
    h_scr[halo + tm:] = _rmsnorm(xn_ref[0], g).astype(BF16)
    h = h_scr[...]
    row = lax.broadcasted_iota(jnp.int32, (rows, 1), 0)
    inside = jnp.logical_and(jnp.logical_or(row >= halo, i > 0),
                             jnp.logical_or(row < halo + tm, i < n - 1))
    for c in range(d // CONV_COL_CHUNK):
        cs = slice(c * CONV_COL_CHUNK, (c + 1) * CONV_COL_CHUNK)
        c_cols = slice(d + c * CONV_COL_CHUNK, d + (c + 1) * CONV_COL_CHUNK)
        h_cols = slice(2 * d + c * CONV_COL_CHUNK, 2 * d + (c + 1) * CONV_COL_CHUNK)
        z = jnp.where(inside, _dot(h, win_ref[:, c_cols]) * _dot(h, win_ref[:, h_cols]), 0.0)
        z_prev = pltpu.roll(z, 1, axis=0)[halo:halo + tm]
        z_next = pltpu.roll(z, rows - 1, axis=0)[halo:halo + tm]
        y = wdw_ref[0:1, cs] * z_prev + wdw_ref[1:2, cs] * z[halo:halo + tm] + wdw_ref[2:3, cs] * z_next
        b_gate = _dot(h_scr[halo:halo + tm], win_ref[:, cs])
        u_scr[:, cs] = (b_gate * y).astype(BF16)
    o_ref[0] = xm + _dot(u_scr[...], wout_ref[...])


def _conv_mixer(x, layer, j, g_norm, w_in, w_dw, w_out):
    b, s, d = x.shape
    tm = WIDE_TILE
    halo = CONV_HALO
    assert s % tm == 0 and tm % halo == 0 and d % CONV_COL_CHUNK == 0
    per = tm // halo
    last = s // halo - 1
    main = pl.BlockSpec((1, tm, d), lambda bi, i: (bi, i, 0))
    prev = pl.BlockSpec((1, halo, d), lambda bi, i: (bi, jnp.maximum(i * per - 1, 0), 0))
    nxt = pl.BlockSpec((1, halo, d), lambda bi, i: (bi, jnp.minimum((i + 1) * per, last), 0))
    est = 4 * d * d * 2 + 4 * tm * d * 4 + 2 * (tm + 2 * halo) * d * 2 + 8 * (tm + 2 * halo) * CONV_COL_CHUNK * 4
    params = [_layer_param(g_norm, layer)] + [_layer_param(a, j) for a in (w_in, w_dw, w_out)]
    return pl.pallas_call(
        _conv_body,
        grid=(b, s // tm),
        in_specs=[main, prev, nxt] + [spec for spec, _ in params],
        out_specs=main,
        out_shape=jax.ShapeDtypeStruct((b, s, d), F32),
        scratch_shapes=[pltpu.VMEM((tm + 2 * halo, d), BF16), pltpu.VMEM((tm, d), BF16)],
        compiler_params=pltpu.CompilerParams(
            dimension_semantics=("parallel", "parallel"), vmem_limit_bytes=_vmem_limit(est)),
        name="conv_mixer",
    )(x, x, x, *[a for _, a in params])


def _log_sigmoid(x):
    return jnp.minimum(x, 0.0) - jnp.log1p(jnp.exp(-jnp.abs(x)))


def _lane_scan(x, op, identity, forward):
    length = x.shape[1]
    lane = lax.broadcasted_iota(jnp.int32, x.shape, 1)
    shift = 1
    while shift < length:
        if forward:
            moved = pltpu.roll(x, shift, axis=1)
            valid = lane >= shift
        else:
            moved = pltpu.roll(x, length - shift, axis=1)
            valid = lane < length - shift
        x = op(x, jnp.where(valid, moved, identity))
        shift *= 2
    return x


def _bf16_pieces(x, n):
    pieces = []
    for _ in range(n):
        piece = x.astype(BF16).astype(F32)
        pieces.append(piece)
        x = x - piece
    return pieces


def _chunk_gate_features(gates, forward):
    nh = N_HEADS
    length = gates.shape[1]
    logf = _log_sigmoid(gates[nh:, :]) * LOG2_E
    b = _lane_scan(logf, jnp.add, 0.0, forward)
    r = gates[0:nh, :] * LOG2_E - b
    cm = _lane_scan(r, jnp.maximum, -jnp.inf, forward)
    g_tot = jnp.broadcast_to(jnp.sum(logf, axis=1, keepdims=True), (nh, length))
    a = g_tot + r
    m_loc = jnp.broadcast_to(jnp.max(a, axis=1, keepdims=True), (nh, length))
    w_loc = jnp.exp2(a - m_loc)
    pieces = _bf16_pieces(-cm, 3) + _bf16_pieces(-b, 3) + _bf16_pieces(w_loc, 2)
    stacked = jnp.concatenate(pieces, axis=0).astype(BF16)
    return jnp.concatenate([r, g_tot, m_loc], axis=0), stacked


def _pieces_to_cols(stacked):
    nh = N_HEADS
    n_rows = stacked.shape[0]
    prow = lax.broadcasted_iota(jnp.int32, (n_rows, 4 * nh), 0)
    pcol = lax.broadcasted_iota(jnp.int32, (n_rows, 4 * nh), 1)
    quantity = (prow >= 3 * nh).astype(jnp.int32) + (prow >= 6 * nh).astype(jnp.int32)
    gather = (pcol == (prow % nh) + nh * quantity).astype(BF16)
    return lax.dot_general(stacked, gather, (((0,), (0,)), ((), ())), preferred_element_type=F32)


def _mlstm_in_body(x_ref, g_ref, win_ref, wgate_t_ref, bgate_t_ref,
                   q_ref, k_ref, v_ref, o_ref, rows_fw_ref, rows_bw_ref, cols_fw_ref, cols_bw_ref):
    d_qk = q_ref.shape[2]
    d_v = v_ref.shape[2]
    h = _rmsnorm(x_ref[0], g_ref[...]).astype(BF16)
    gates_t = lax.dot_general(wgate_t_ref[...], h, (((1,), (1,)), ((), ())),
                              preferred_element_type=F32) + bgate_t_ref[...]
    half = 2 * N_HEADS
    stacked = []
    for c in range(rows_fw_ref.shape[1]):
        sl = slice(c * SCAN_CHUNK, (c + 1) * SCAN_CHUNK)
        rows_fw_ref[0, c], st_fw = _chunk_gate_features(gates_t[0:half, sl], True)
        rows_bw_ref[0, c], st_bw = _chunk_gate_features(gates_t[half:, sl], False)
        stacked.append((sl, st_fw, st_bw))
    q_ref[0] = _dot(h, win_ref[:, 0:d_qk]).astype(BF16)
    k_ref[0] = (_dot(h, win_ref[:, d_qk:2 * d_qk]) * (DH_QK ** -0.5)).astype(BF16)
    v_ref[0] = _dot(h, win_ref[:, 2 * d_qk:2 * d_qk + d_v]).astype(BF16)
    o_ref[0] = _dot(h, win_ref[:, 2 * d_qk + d_v:])
    for sl, st_fw, st_bw in stacked:
        cols_fw_ref[0, sl, :] = _pieces_to_cols(st_fw)
        cols_bw_ref[0, sl, :] = _pieces_to_cols(st_bw)


def _mlstm_in(x, layer, j, g_norm, w_in, w_gate_t, b_gate_t):
    b, s, d = x.shape
    tm = WIDE_TILE
    d_qk = N_HEADS * DH_QK
    d_v = N_HEADS * DH_V
    n_gate = 4 * N_HEADS
    n_rows = 3 * N_HEADS
    cpt = tm // SCAN_CHUNK
    assert s % tm == 0 and tm % SCAN_CHUNK == 0
    tok = lambda width: pl.BlockSpec((1, tm, width), lambda bi, i: (bi, i, 0))
    rows_spec = pl.BlockSpec((1, cpt, n_rows, SCAN_CHUNK), lambda bi, i: (bi, i, 0, 0))
    rows_shape = jax.ShapeDtypeStruct((b, s // SCAN_CHUNK, n_rows, SCAN_CHUNK), F32)
    cols_shape = jax.ShapeDtypeStruct((b, s, n_gate), F32)
    est = d * (2 * d_qk + 2 * d_v) * 2 + 2 * tm * (d * 4 + (2 * d_qk + d_v) * 2 + d_v * 4) + 4 * tm * d * 4
    params = [_layer_param(g_norm, layer)] + [_layer_param(a, j) for a in (w_in, w_gate_t, b_gate_t)]
    return pl.pallas_call(
        _mlstm_in_body,
        grid=(b, s // tm),
        in_specs=[tok(d)] + [spec for spec, _ in params],
        out_specs=[tok(d_qk), tok(d_qk), tok(d_v), tok(d_v), rows_spec, rows_spec, tok(n_gate), tok(n_gate)],
        out_shape=[jax.ShapeDtypeStruct((b, s, d_qk), BF16), jax.ShapeDtypeStruct((b, s, d_qk), BF16),
                   jax.ShapeDtypeStruct((b, s, d_v), BF16), jax.ShapeDtypeStruct((b, s, d_v), F32),
                   rows_shape, rows_shape, cols_shape, cols_shape],
        compiler_params=pltpu.CompilerParams(
            dimension_semantics=("parallel", "parallel"), vmem_limit_bytes=_vmem_limit(est)),
        name="mlstm_in",
    )(x, *[a for _, a in params])


def _scan_chunk(forward, q, k, v, rows, cols, c_ref, m_ref, h_ref, row0):
    nh = N_HEADS
    length = q.shape[0]
    ti = lax.broadcasted_iota(jnp.int32, (length, length), 0)
    ji = lax.broadcasted_iota(jnp.int32, (length, length), 1)
    seen = (ji <= ti) if forward else (ji >= ti)

    r = rows[0:nh, :]
    g_tot = rows[nh:2 * nh, :]
    m_loc = rows[2 * nh:, :]
    m_prev = m_ref[...]
    m_new = jnp.maximum(g_tot + m_prev, m_loc)
    s_old = jnp.exp2(g_tot + m_prev - m_new)
    s_loc = jnp.exp2(m_loc - m_new)
    tn = (((0,), (0,)), ((), ()))
    nt = (((1,), (1,)), ((), ()))

    qk_tile = (length, 2 * DH_QK)
    low = lax.broadcasted_iota(jnp.int32, qk_tile, 1) < DH_QK
    ones_blk = jnp.ones((length, DH_V), BF16)

    def v_aug(hd):
        return jnp.concatenate([v[:, DH_V * hd:DH_V * (hd + 1)], ones_blk], axis=1)

    def scores(p):
        qp = q[:, 2 * DH_QK * p:2 * DH_QK * (p + 1)]
        kp = k[:, 2 * DH_QK * p:2 * DH_QK * (p + 1)]
        qms = [jnp.where(low, qp, jnp.zeros_like(qp)), jnp.where(low, jnp.zeros_like(qp), qp)]
        s2 = lax.dot_general(jnp.concatenate(qms, axis=0), kp, nt, preferred_element_type=F32)
        return qms, [s2[0:length], s2[length:]]

    def outputs(hd, qm, s):
        p = hd // 2
        mp = m_prev[hd:hd + 1, 0:V7X_LANES]
        neg_m = jnp.minimum(jnp.broadcast_to(cols[:, hd:hd + 1], (length, V7X_LANES)), -mp)
        neg_m_keys = jnp.concatenate([neg_m] * (length // V7X_LANES), axis=1)
        sc = s * jnp.exp2(jnp.where(seen, neg_m_keys + r[hd:hd + 1, :], -jnp.inf))
        wq = qm.astype(F32) * jnp.exp2(neg_m + mp)
        p_ext = jnp.concatenate([sc.astype(BF16), wq.astype(BF16)], axis=1)
        nd = _dot(p_ext, jnp.concatenate([v_aug(hd), c_ref[p].astype(BF16)], axis=0))
        neg_b = jnp.broadcast_to(cols[:, nh + hd:nh + hd + 1], (length, DH_V))
        floor = jnp.exp2(neg_b + neg_m)
        den = jnp.maximum(jnp.abs(nd[:, DH_V:]), floor)
        h_ref[0, pl.ds(row0, length), DH_V * hd:DH_V * (hd + 1)] = nd[:, 0:DH_V] / den

    def update_state(p):
        kf = k[:, 2 * DH_QK * p:2 * DH_QK * (p + 1)].astype(F32)
        w0 = jnp.broadcast_to(cols[:, 2 * nh + 2 * p:2 * nh + 2 * p + 1], qk_tile)
        w1 = jnp.broadcast_to(cols[:, 2 * nh + 2 * p + 1:2 * nh + 2 * p + 2], qk_tile)
        kw = jnp.concatenate([kf * jnp.where(low, w0, 0.0), kf * jnp.where(low, 0.0, w1)], axis=0)
        c_loc = lax.dot_general(kw.astype(BF16), jnp.concatenate([v_aug(2 * p), v_aug(2 * p + 1)], axis=0), tn,
                                preferred_element_type=F32)

        def per_row(scale):
            halves = [jnp.broadcast_to(scale[hd:hd + 1, 0:DH_V], (DH_QK, DH_V)) for hd in (2 * p, 2 * p + 1)]
            half = jnp.concatenate(halves, axis=0)
            return jnp.concatenate([half, half], axis=1)

        c_ref[p] = per_row(s_old) * c_ref[p] + per_row(s_loc) * c_loc

    for p in range(nh // 2):
        qms, ss = scores(p)
        outputs(2 * p, qms[0], ss[0])
        outputs(2 * p + 1, qms[1], ss[1])
        update_state(p)
    m_ref[...] = m_new


def _mlstm_scan_body(qf_ref, kf_ref, vf_ref, rowsf_ref, colsf_ref, qb_ref, kb_ref, vb_ref, rowsb_ref, colsb_ref,
                     hf_ref, hb_ref, cf_scr, cb_scr, mf_scr, mb_scr):
    @pl.when(pl.program_id(1) == 0)
    def _():
        cf_scr[...] = jnp.zeros_like(cf_scr)
        cb_scr[...] = jnp.zeros_like(cb_scr)
        mf_scr[...] = jnp.zeros_like(mf_scr)
        mb_scr[...] = jnp.zeros_like(mb_scr)

    length = SCAN_CHUNK
    n_chunks = qf_ref.shape[1] // length

    for c in range(n_chunks):
        rf = c * length
        _scan_chunk(True, qf_ref[0, pl.ds(rf, length), :], kf_ref[0, pl.ds(rf, length), :],
                    vf_ref[0, pl.ds(rf, length), :], rowsf_ref[0, c], colsf_ref[0, pl.ds(rf, length), :],
                    cf_scr, mf_scr, hf_ref, rf)
        cb = n_chunks - 1 - c
        rb = cb * length
        _scan_chunk(False, qb_ref[0, pl.ds(rb, length), :], kb_ref[0, pl.ds(rb, length), :],
                    vb_ref[0, pl.ds(rb, length), :], rowsb_ref[0, cb], colsb_ref[0, pl.ds(rb, length), :],
                    cb_scr, mb_scr, hb_ref, rb)


def _mlstm_scan(q, k, v, rows_fw, rows_bw, cols_fw, cols_bw):
    b, s, d_qk = q.shape
    d_v = v.shape[2]
    tb = SCAN_BLOCK
    nb = s // tb
    cpb = tb // SCAN_CHUNK
    n_rows = rows_fw.shape[2]
    n_cols = cols_fw.shape[2]
    assert s % tb == 0 and tb % SCAN_CHUNK == 0
    assert SCAN_CHUNK % V7X_LANES == 0 and 2 * DH_QK == V7X_LANES and DH_V == V7X_LANES
    fw = lambda width: pl.BlockSpec((1, tb, width), lambda bi, i: (bi, i, 0))
    bw = lambda width: pl.BlockSpec((1, tb, width), lambda bi, i: (bi, nb - 1 - i, 0))
    rows_f = pl.BlockSpec((1, cpb, n_rows, SCAN_CHUNK), lambda bi, i: (bi, i, 0, 0))
    rows_b = pl.BlockSpec((1, cpb, n_rows, SCAN_CHUNK), lambda bi, i: (bi, nb - 1 - i, 0, 0))
    state = pltpu.VMEM((N_HEADS // 2, 2 * DH_QK, 2 * DH_V), F32)
    stab = pltpu.VMEM((N_HEADS, SCAN_CHUNK), F32)
    est = 2 * 2 * tb * ((2 * d_qk + d_v) * 2 + d_v * 4 + (n_rows + V7X_LANES) * 4) + 16 * SCAN_CHUNK * 1024 * 4
    return pl.pallas_call(
        _mlstm_scan_body,
        grid=(b, nb),
        in_specs=[fw(d_qk), fw(d_qk), fw(d_v), rows_f, fw(n_cols),
                  bw(d_qk), bw(d_qk), bw(d_v), rows_b, bw(n_cols)],
        out_specs=[fw(d_v), bw(d_v)],
        out_shape=[jax.ShapeDtypeStruct((b, s, d_v), F32), jax.ShapeDtypeStruct((b, s, d_v), F32)],
        scratch_shapes=[state, state, stab, stab],
        compiler_params=pltpu.CompilerParams(
            dimension_semantics=("arbitrary", "arbitrary"), vmem_limit_bytes=_vmem_limit(est)),
        name="mlstm_scan",
    )(q, k, v, rows_fw, cols_fw, q, k, v, rows_bw, cols_bw)


def _trunk(x, p):
    b, s, d = x.shape
    flat = lambda a: a.reshape(b * s, a.shape[-1])
    depth = p["g_ffn1"].shape[0]
    for layer in range(depth):
        j = layer // 2
        x = _ffn(flat(x), layer, p["g_ffn1"], p["w_ffn1_gate"], p["w_ffn1_up"], p["w_ffn1_down"]).reshape(b, s, d)
        mlstm = None
        if layer % 2 == 0:
            x = _conv_mixer(x, layer, j, p["g_mix"], p["w_conv_in"], p["w_conv_dw"], p["w_conv_out"])
        else:
            q, k, v, o, rows_fw, rows_bw, cols_fw, cols_bw = _mlstm_in(
                x, layer, j, p["g_mix"], p["w_mlstm_in"], p["w_mlstm_gate_t"], p["b_mlstm_gate_t"])
            h_fw, h_bw = _mlstm_scan(q, k, v, rows_fw, rows_bw, cols_fw, cols_bw)
            mlstm = (flat(h_fw), flat(h_bw), flat(o), p["g_mlstm_head"], p["w_mlstm_out"], j)
        g_final = p["g_final"] if layer == depth - 1 else None
        x = _ffn(flat(x), layer, p["g_ffn2"], p["w_ffn2_gate"], p["w_ffn2_up"], p["w_ffn2_down"],
                 mlstm, g_final).reshape(b, s, d)
    return x


def kernel(x_prompt, x_sample, g_ffn1, w_ffn1_gate, w_ffn1_up, w_ffn1_down, g_mix, w_conv_in, w_conv_dw, w_conv_out, w_mlstm_in, w_mlstm_gate, b_mlstm_gate, g_mlstm_head, w_mlstm_out, g_ffn2, w_ffn2_gate, w_ffn2_up, w_ffn2_down, g_final):
    bf = lambda w: w.astype(BF16)
    p = dict(
        g_ffn1=g_ffn1, w_ffn1_gate=bf(w_ffn1_gate), w_ffn1_up=bf(w_ffn1_up), w_ffn1_down=bf(w_ffn1_down),
        g_mix=g_mix, w_conv_in=bf(w_conv_in), w_conv_dw=w_conv_dw, w_conv_out=bf(w_conv_out),
        w_mlstm_in=bf(w_mlstm_in), w_mlstm_gate_t=bf(jnp.swapaxes(w_mlstm_gate, 1, 2)),
        b_mlstm_gate_t=b_mlstm_gate[:, :, None],
        g_mlstm_head=g_mlstm_head, w_mlstm_out=bf(w_mlstm_out),
        g_ffn2=g_ffn2, w_ffn2_gate=bf(w_ffn2_gate), w_ffn2_up=bf(w_ffn2_up), w_ffn2_down=bf(w_ffn2_down),
        g_final=g_final)
    return _trunk(x_prompt, p), _trunk(x_sample, p)
```
